```python
import math
import jax, jax.numpy as jnp
from jax import lax
import numpy as np

D_MODEL = 2048
BATCH = 4
SEQ = 4096
DEPTH = 4

ATTN_HEADS = 16
ATTN_KV_HEADS = 4
HEAD_DIM = 64
WINDOW = 128
ATTN_WIDTH = ATTN_HEADS * HEAD_DIM
KV_WIDTH = ATTN_KV_HEADS * HEAD_DIM
D_INNER = D_MODEL
SSM_HEAD_DIM = 64
SSM_HEADS = D_INNER // SSM_HEAD_DIM
SSM_GROUPS = 4
D_STATE = 128
CONV_WIDTH = 4
CHUNK = 128
CONV_CH = D_INNER + 2 * SSM_GROUPS * D_STATE
D_FF = 256 * ((8 * D_MODEL // 3 + 255) // 256)
N_MOD = 9
NORM_EPS = 1e-6
P_IN = ATTN_WIDTH + 2 * KV_WIDTH + D_INNER + CONV_CH + SSM_HEADS + 2 * D_MODEL

kernel_name = "hybrid_swa_ssd_macaron_block"


def _rms_normalize(x):
    xf = x.astype(jnp.float32)
    return xf * lax.rsqrt(jnp.mean(xf * xf, axis=-1, keepdims=True) + NORM_EPS)


def rms_norm(x, g):
    return (_rms_normalize(x) * g.astype(jnp.float32)).astype(x.dtype)


def modulate(n, shift, scale):
    return n * (1.0 + scale) + shift


def swiglu(h, w_gate, w_up, w_down):
    return (jax.nn.silu(h @ w_gate) * (h @ w_up)) @ w_down


def alibi_slopes(n_heads):
    return jnp.exp2(-8.0 * jnp.arange(1, n_heads + 1, dtype=jnp.float32) / n_heads)


def sliding_window_attention(q, k, v, sinks):
    b, L = q.shape[0], q.shape[1]
    nb = L // WINDOW
    grp = ATTN_HEADS // ATTN_KV_HEADS
    qb = q.reshape(b, nb, WINDOW, ATTN_KV_HEADS, grp, HEAD_DIM)

    def band(t):
        pad = jnp.zeros((b, WINDOW, ATTN_KV_HEADS, HEAD_DIM), t.dtype)
        tb = jnp.concatenate([pad, t], axis=1).reshape(b, nb + 1, WINDOW, ATTN_KV_HEADS, HEAD_DIM)
        return jnp.concatenate([tb[:, :-1], tb[:, 1:]], axis=2)

    kb, vb = band(k), band(v)
    s = jnp.einsum('bnqkgd,bnskd->bnkgqs', qb, kb,
                   preferred_element_type=jnp.float32) * (HEAD_DIM ** -0.5)
    qpos = jnp.arange(WINDOW)[:, None]
    kpos = jnp.arange(2 * WINDOW)[None, :]
    dist = WINDOW + qpos - kpos
    abs_k = (jnp.arange(nb)[:, None, None] - 1) * WINDOW + kpos[None]
    valid = (dist >= 0)[None] & (dist < WINDOW)[None] & (abs_k >= 0)
    slopes = alibi_slopes(ATTN_HEADS).reshape(ATTN_KV_HEADS, grp)
    s = s - slopes[:, :, None, None] * dist.astype(jnp.float32)
    s = jnp.where(valid[None, :, None, None], s, -jnp.inf)
    sink = sinks.astype(jnp.float32).reshape(ATTN_KV_HEADS, grp)[:, :, None, None]
    m = jnp.maximum(jnp.max(s, axis=-1, keepdims=True), sink)
    e = jnp.exp(s - m)
    p = (e / (jnp.sum(e, axis=-1, keepdims=True) + jnp.exp(sink - m))).astype(v.dtype)
    o = jnp.einsum('bnkgqs,bnskd->bnqkgd', p, vb)
    return o.reshape(b, L, ATTN_WIDTH)


def causal_depthwise_conv(u, w, bias):
    ch = u.shape[-1]
    out = lax.conv_general_dilated(u, w[:, None, :].astype(u.dtype), window_strides=(1,),
                                   padding=[(CONV_WIDTH - 1, 0)],
                                   dimension_numbers=('NWC', 'WIO', 'NWC'),
                                   feature_group_count=ch)
    return out + bias


def ssd_chunked(x, dt, a, bmat, cmat):
    b, L = x.shape[0], x.shape[1]
    nc = L // CHUNK
    R = SSM_HEADS // SSM_GROUPS
    xc = x.astype(jnp.float32).reshape(b, nc, CHUNK, SSM_GROUPS, R, SSM_HEAD_DIM)
    bc = bmat.astype(jnp.float32).reshape(b, nc, CHUNK, SSM_GROUPS, D_STATE)
    cc = cmat.astype(jnp.float32).reshape(b, nc, CHUNK, SSM_GROUPS, D_STATE)
    dtc = dt.reshape(b, nc, CHUNK, SSM_GROUPS, R)
    a_cs = jnp.cumsum(dtc * a.reshape(SSM_GROUPS, R), axis=2)
    xdt = xc * dtc[..., None]

    acs_t = jnp.moveaxis(a_cs, 2, -1)
    causal = jnp.tril(jnp.ones((CHUNK, CHUNK), dtype=bool))
    decay = jnp.exp(jnp.where(causal, acs_t[..., :, None] - acs_t[..., None, :], -jnp.inf))
    cb = jnp.einsum('bclgn,bcsgn->bcgls', cc, bc)
    y_diag = jnp.einsum('bcgrls,bcsgrp->bclgrp', cb[:, :, :, None] * decay, xdt)

    decay_states = jnp.exp(a_cs[:, :, -1:] - a_cs)
    states = jnp.einsum('bclgn,bclgrp->bcgrpn', bc, xdt * decay_states[..., None])
    chunk_decay = jnp.exp(a_cs[:, :, -1])

    def step(carry, inp):
        st, dec = inp
        return carry * dec[..., None, None] + st, carry

    init = jnp.zeros((b, SSM_GROUPS, R, SSM_HEAD_DIM, D_STATE), jnp.float32)
    _, prev = lax.scan(step, init, (jnp.swapaxes(states, 0, 1), jnp.swapaxes(chunk_decay, 0, 1)))
    prev = jnp.swapaxes(prev, 0, 1)

    y_off = jnp.einsum('bclgn,bcgrpn->bclgrp', cc, prev) * jnp.exp(a_cs)[..., None]
    return (y_diag + y_off).reshape(b, L, SSM_HEADS, SSM_HEAD_DIM)


def token_mixer(h, w_in, sinks, conv_w, conv_b, dt_bias, a_log, d_skip, ssm_norm,
                w_attn_out, w_ssm_out, w_out):
    b, L, _ = h.shape
    splits = np.cumsum([ATTN_WIDTH, KV_WIDTH, KV_WIDTH, D_INNER, CONV_CH, SSM_HEADS, D_MODEL]).tolist()
    q, k, v, z, xbc, dt_raw, g_attn, g_ssd = jnp.split(h @ w_in, splits, axis=-1)

    y_a = sliding_window_attention(q.reshape(b, L, ATTN_HEADS, HEAD_DIM),
                                   k.reshape(b, L, ATTN_KV_HEADS, HEAD_DIM),
                                   v.reshape(b, L, ATTN_KV_HEADS, HEAD_DIM), sinks) @ w_attn_out

    xbc = jax.nn.silu(causal_depthwise_conv(xbc, conv_w, conv_b))
    xs, bm, cm = jnp.split(xbc, [D_INNER, D_INNER + SSM_GROUPS * D_STATE], axis=-1)
    xs = xs.reshape(b, L, SSM_HEADS, SSM_HEAD_DIM)
    dt = jax.nn.softplus(dt_raw.astype(jnp.float32) + dt_bias.astype(jnp.float32))
    a = -jnp.exp(a_log.astype(jnp.float32))
    y = ssd_chunked(xs, dt, a, bm.reshape(b, L, SSM_GROUPS, D_STATE),
                    cm.reshape(b, L, SSM_GROUPS, D_STATE))
    y = y + d_skip.astype(jnp.float32)[:, None] * xs.astype(jnp.float32)
    y = y.reshape(b, L, D_INNER) * jax.nn.silu(z.astype(jnp.float32))
    y = _rms_normalize(y.reshape(b, L, SSM_GROUPS, D_INNER // SSM_GROUPS)).reshape(b, L, D_INNER)
    y_s = (y * ssm_norm.astype(jnp.float32)).astype(h.dtype) @ w_ssm_out

    merged = jax.nn.sigmoid(g_attn) * y_a + jax.nn.sigmoid(g_ssd) * y_s
    return merged @ w_out


def setup_inputs(seed: int = 0) -> dict:
    key = jax.random.key(seed)
    ks = jax.random.split(key, 20)
    nrm = jax.random.normal
    f32 = jnp.float32
    dt0 = jnp.exp(jax.random.uniform(ks[13], (DEPTH, SSM_HEADS), f32,
                                     minval=math.log(1e-3), maxval=math.log(1e-1)))
    return {
        "x": nrm(ks[0], (BATCH, SEQ, D_MODEL), f32),
        "c": nrm(ks[1], (BATCH, D_MODEL), f32),
        "w_mod": nrm(ks[2], (DEPTH, D_MODEL, N_MOD * D_MODEL), f32) * D_MODEL ** -0.5,
        "b_mod": 0.02 * nrm(ks[3], (DEPTH, N_MOD * D_MODEL), f32),
        "norm_pre": 1.0 + 0.05 * nrm(ks[4], (DEPTH, 3, D_MODEL), f32),
        "norm_post": 1.0 + 0.05 * nrm(ks[5], (DEPTH, 3, D_MODEL), f32),
        "w_ffn_gate": nrm(ks[6], (DEPTH, 2, D_MODEL, D_FF), f32) * D_MODEL ** -0.5,
        "w_ffn_up": nrm(ks[7], (DEPTH, 2, D_MODEL, D_FF), f32) * D_MODEL ** -0.5,
        "w_ffn_down": nrm(ks[8], (DEPTH, 2, D_FF, D_MODEL), f32) * D_FF ** -0.5,
        "w_in": nrm(ks[9], (DEPTH, D_MODEL, P_IN), f32) * D_MODEL ** -0.5,
        "attn_sinks": 0.5 * nrm(ks[10], (DEPTH, ATTN_HEADS), f32),
        "conv_w": nrm(ks[11], (DEPTH, CONV_WIDTH, CONV_CH), f32) * CONV_WIDTH ** -0.5,
        "conv_b": 0.02 * nrm(ks[12], (DEPTH, CONV_CH), f32),
        "dt_bias": dt0 + jnp.log(-jnp.expm1(-dt0)),
        "a_log": jnp.log(jax.random.uniform(ks[14], (DEPTH, SSM_HEADS), f32, minval=1.0, maxval=16.0)),
        "d_skip": 1.0 + 0.1 * nrm(ks[15], (DEPTH, SSM_HEADS), f32),
        "ssm_norm": 1.0 + 0.05 * nrm(ks[16], (DEPTH, D_INNER), f32),
        "w_attn_out": nrm(ks[17], (DEPTH, ATTN_WIDTH, D_MODEL), f32) * ATTN_WIDTH ** -0.5,
        "w_ssm_out": nrm(ks[18], (DEPTH, D_INNER, D_MODEL), f32) * D_INNER ** -0.5,
        "w_out": nrm(ks[19], (DEPTH, D_MODEL, D_MODEL), f32) * D_MODEL ** -0.5,
    }


def reference(x, c, w_mod, b_mod, norm_pre, norm_post, w_ffn_gate, w_ffn_up, w_ffn_down,
              w_in, attn_sinks, conv_w, conv_b, dt_bias, a_log, d_skip, ssm_norm,
              w_attn_out, w_ssm_out, w_out):
    b = x.shape[0]
    c_act = jax.nn.silu(c)
    for l in range(DEPTH):
        mod = (c_act @ w_mod[l] + b_mod[l]).reshape(b, 1, N_MOD, D_MODEL)
        sh1, sc1, g1, sh2, sc2, g2, sh3, sc3, g3 = [mod[:, :, i] for i in range(N_MOD)]

        n = modulate(rms_norm(x, norm_pre[l, 0]), sh1, sc1)
        f = swiglu(n, w_ffn_gate[l, 0], w_ffn_up[l, 0], w_ffn_down[l, 0])
        x = x + 0.5 * g1 * rms_norm(f, norm_post[l, 0])

        n = modulate(rms_norm(x, norm_pre[l, 1]), sh2, sc2)
        t = token_mixer(n, w_in[l], attn_sinks[l], conv_w[l], conv_b[l], dt_bias[l], a_log[l],
                        d_skip[l], ssm_norm[l], w_attn_out[l], w_ssm_out[l], w_out[l])
        x = x + g2 * rms_norm(t, norm_post[l, 1])

        n = modulate(rms_norm(x, norm_pre[l, 2]), sh3, sc3)
        f = swiglu(n, w_ffn_gate[l, 1], w_ffn_up[l, 1], w_ffn_down[l, 1])
        x = x + 0.5 * g3 * rms_norm(f, norm_post[l, 2])
    return x
```

```python
import functools

import jax
import jax.numpy as jnp
from jax import lax
from jax.experimental import pallas as pl
from jax.experimental.pallas import tpu as pltpu

F32 = jnp.float32
BF16 = jnp.bfloat16

D_MODEL = 2048
DEPTH = 4
ATTN_HEADS = 16
ATTN_KV_HEADS = 4
HEAD_DIM = 64
WINDOW = 128
ATTN_WIDTH = ATTN_HEADS * HEAD_DIM
KV_WIDTH = ATTN_KV_HEADS * HEAD_DIM
D_INNER = D_MODEL
SSM_HEAD_DIM = 64
SSM_HEADS = D_INNER // SSM_HEAD_DIM
SSM_GROUPS = 4
D_STATE = 128
CONV_WIDTH = 4
CHUNK = 128
BC_WIDTH = SSM_GROUPS * D_STATE
CONV_CH = D_INNER + 2 * BC_WIDTH
D_FF = 5632
N_MOD = 9
NORM_EPS = 1e-6
GROUP_WIDTH = D_INNER // SSM_GROUPS
HEADS_PER_GROUP = SSM_HEADS // SSM_GROUPS
Q_PER_KV = ATTN_HEADS // ATTN_KV_HEADS

LANES = 128
HALO_ROWS = 8
VMEM_LIMIT = 56 * 1024 * 1024

COL_Z = 0
COL_GA = COL_Z + D_INNER
COL_GS = COL_GA + D_MODEL
COL_XS = COL_GS + D_MODEL
COL_Q = COL_XS + D_INNER
COL_B = COL_Q + ATTN_WIDTH
COL_C = COL_B + BC_WIDTH
COL_KD = COL_C + BC_WIDTH
COL_VD = COL_KD + 2 * KV_WIDTH
P_MAIN = COL_VD + 2 * KV_WIDTH

TM_FFN = 512
TF_FFN = 512
TM_PROJ = 512
TN_PROJ = 1024
TQ_ATTN = 512
TM_MERGE = 256
TN_MOD = 1024


def _dot(a, b):
    return jnp.dot(a, b, preferred_element_type=F32)


def _dot_nt(a, b):
    return lax.dot_general(a, b, (((1,), (1,)), ((), ())), preferred_element_type=F32)


def _sigmoid(x):
    return 1.0 / (1.0 + jnp.exp(-x))


def _silu(x):
    return x * _sigmoid(x)


def _rms_normalize(x):
    return x * lax.rsqrt(jnp.mean(x * x, axis=-1, keepdims=True) + NORM_EPS)


def _split3(v):
    hi = v.astype(BF16)
    r1 = v - hi.astype(F32)
    mid = r1.astype(BF16)
    lo = (r1 - mid.astype(F32)).astype(BF16)
    return hi, mid, lo


def _dot_exact_rhs(v, m):
    hi, mid, lo = _split3(v)
    return _dot(hi, m) + _dot(mid, m) + _dot(lo, m)


def _dot_exact_lhs(m, v):
    hi, mid, lo = _split3(v)
    return _dot(m, hi) + _dot(m, mid) + _dot(m, lo)


def _mod_kernel(c_ref, w_ref, b_ref, o_ref):
    c = c_ref[...]
    o_ref[0] = _dot(_silu(c).astype(BF16), w_ref[0].astype(BF16)) + b_ref[0]


def _mod_all(c_pad, w_mod, b_mod):
    rows = c_pad.shape[0]
    n = N_MOD * D_MODEL
    return pl.pallas_call(
        _mod_kernel,
        grid=(DEPTH, n // TN_MOD),
        in_specs=[
            pl.BlockSpec((rows, D_MODEL), lambda l, j: (0, 0)),
            pl.BlockSpec((1, D_MODEL, TN_MOD), lambda l, j: (l, 0, j)),
            pl.BlockSpec((1, 1, TN_MOD), lambda l, j: (l, 0, j)),
        ],
        out_specs=pl.BlockSpec((1, rows, TN_MOD), lambda l, j: (l, 0, j)),
        out_shape=jax.ShapeDtypeStruct((DEPTH, rows, n), F32),
        compiler_params=pltpu.CompilerParams(
            dimension_semantics=("arbitrary", "arbitrary"), vmem_limit_bytes=VMEM_LIMIT),
        name="mod_proj",
    )(c_pad, w_mod, b_mod.reshape(DEPTH, 1, n))


def _ffn_kernel(x_ref, mod_ref, npre_ref, npost_ref, wg_ref, wu_ref, wd_ref, o_ref,
                h_ref, acc_ref, *, mod_base):
    j = pl.program_id(1)

    @pl.when(j == 0)
    def _():
        n = _rms_normalize(x_ref[...]) * npre_ref[...]
        shift = mod_ref[0, mod_base:mod_base + 1, :]
        scale = mod_ref[0, mod_base + 1:mod_base + 2, :]
        h_ref[...] = (n * (1.0 + scale) + shift).astype(BF16)
        acc_ref[...] = jnp.zeros_like(acc_ref)

    h = h_ref[...]
    gate = _dot(h, wg_ref[...])
    up = _dot(h, wu_ref[...])
    acc_ref[...] += _dot((_silu(gate) * up).astype(BF16), wd_ref[...])

    @pl.when(j == pl.num_programs(1) - 1)
    def _():
        r = _rms_normalize(acc_ref[...]) * npost_ref[...]
        g = mod_ref[0, mod_base + 2:mod_base + 3, :]
        o_ref[...] = x_ref[...] + 0.5 * g * r


def _ffn(x, mod_l, npre, npost, wg, wu, wd, layer, which, seq):
    t = x.shape[0]
    tiles_per_seq = seq // TM_FFN
    mod_base = 0 if which == 0 else 6
    kern = functools.partial(_ffn_kernel, mod_base=mod_base)
    return pl.pallas_call(
        kern,
        grid=(t // TM_FFN, D_FF // TF_FFN),
        in_specs=[
            pl.BlockSpec((TM_FFN, D_MODEL), lambda i, j: (i, 0)),
            pl.BlockSpec((1, N_MOD, D_MODEL), lambda i, j: (i // tiles_per_seq, 0, 0)),
            pl.BlockSpec((1, D_MODEL), lambda i, j: (0, 0)),
            pl.BlockSpec((1, D_MODEL), lambda i, j: (0, 0)),
            pl.BlockSpec((None, None, D_MODEL, TF_FFN), lambda i, j: (layer, which, 0, j)),
            pl.BlockSpec((None, None, D_MODEL, TF_FFN), lambda i, j: (layer, which, 0, j)),
            pl.BlockSpec((None, None, TF_FFN, D_MODEL), lambda i, j: (layer, which, j, 0)),
        ],
        out_specs=pl.BlockSpec((TM_FFN, D_MODEL), lambda i, j: (i, 0)),
        out_shape=jax.ShapeDtypeStruct((t, D_MODEL), F32),
        scratch_shapes=[pltpu.VMEM((TM_FFN, D_MODEL), BF16), pltpu.VMEM((TM_FFN, D_MODEL), F32)],
        compiler_params=pltpu.CompilerParams(
            dimension_semantics=("arbitrary", "arbitrary"), vmem_limit_bytes=VMEM_LIMIT),
        name="ffn",
    )(x, mod_l, npre, npost, wg, wu, wd)


def _inproj_kernel(x_ref, mod_ref, npre_ref, w_ref, wdt_ref, o_ref, dt_ref, h_ref):
    j = pl.program_id(1)

    @pl.when(j == 0)
    def _():
        n = _rms_normalize(x_ref[...]) * npre_ref[...]
        h = (n * (1.0 + mod_ref[0, 4:5, :]) + mod_ref[0, 3:4, :]).astype(BF16)
        h_ref[...] = h
        dt_ref[...] = _dot(h, wdt_ref[...])

    o_ref[...] = _dot(h_ref[...], w_ref[...]).astype(BF16)


def _inproj(x, mod_l, npre, w_main, w_dt, layer, seq):
    t = x.shape[0]
    tiles_per_seq = seq // TM_PROJ
    return pl.pallas_call(
        _inproj_kernel,
        grid=(t // TM_PROJ, P_MAIN // TN_PROJ),
        in_specs=[
            pl.BlockSpec((TM_PROJ, D_MODEL), lambda i, j: (i, 0)),
            pl.BlockSpec((1, N_MOD, D_MODEL), lambda i, j: (i // tiles_per_seq, 0, 0)),
            pl.BlockSpec((1, D_MODEL), lambda i, j: (0, 0)),
            pl.BlockSpec((None, D_MODEL, TN_PROJ), lambda i, j: (layer, 0, j)),
            pl.BlockSpec((None, D_MODEL, LANES), lambda i, j: (layer, 0, 0)),
        ],
        out_specs=[
            pl.BlockSpec((TM_PROJ, TN_PROJ), lambda i, j: (i, j)),
            pl.BlockSpec((TM_PROJ, LANES), lambda i, j: (i, 0)),
        ],
        out_shape=[jax.ShapeDtypeStruct((t, P_MAIN), BF16), jax.ShapeDtypeStruct((t, LANES), F32)],
        scratch_shapes=[pltpu.VMEM((TM_PROJ, D_MODEL), BF16)],
        compiler_params=pltpu.CompilerParams(
            dimension_semantics=("arbitrary", "arbitrary"), vmem_limit_bytes=VMEM_LIMIT),
        name="in_proj",
    )(x, mod_l, npre, w_main, w_dt)


def _attn_kernel(sink_ref, q_ref, kc_ref, kp_ref, vc_ref, vp_ref, o_ref, kk_ref, vlo_ref, vhi_ref):
    i = pl.program_id(1)
    lo = lax.broadcasted_iota(jnp.int32, (1, 2 * KV_WIDTH), 1) % LANES < HEAD_DIM
    zero = jnp.zeros((), BF16)
    kk_ref[0:WINDOW, :] = kp_ref[...]
    kk_ref[WINDOW:, :] = kc_ref[...]
    vp = vp_ref[...]
    vc = vc_ref[...]
    vlo_ref[0:WINDOW, :] = jnp.where(lo, vp, zero)
    vlo_ref[WINDOW:, :] = jnp.where(lo, vc, zero)
    vhi_ref[0:WINDOW, :] = jnp.where(lo, zero, vp)
    vhi_ref[WINDOW:, :] = jnp.where(lo, zero, vc)

    row = lax.broadcasted_iota(jnp.int32, (WINDOW, 2 * WINDOW), 0)
    col = lax.broadcasted_iota(jnp.int32, (WINDOW, 2 * WINDOW), 1)
    dist = WINDOW + row - col
    band = (dist >= 0) & (dist < WINDOW)
    distf = dist.astype(F32)
    lo1 = lax.broadcasted_iota(jnp.int32, (1, LANES), 1) < HEAD_DIM

    def block(n, carry):
        r0 = pl.multiple_of(n * WINDOW, WINDOW)
        first_key = jnp.where((i == 0) & (n == 0), WINDOW, 0)
        valid = band & (col >= first_key)
        for g in range(ATTN_KV_HEADS):
            kk = kk_ref[pl.ds(r0, 2 * WINDOW), g * LANES:(g + 1) * LANES]
            vlo = vlo_ref[pl.ds(r0, 2 * WINDOW), g * LANES:(g + 1) * LANES]
            vhi = vhi_ref[pl.ds(r0, 2 * WINDOW), g * LANES:(g + 1) * LANES]
            outs = []
            for pair in range(Q_PER_KV // 2):
                c0 = (g * Q_PER_KV + 2 * pair) * HEAD_DIM
                q2 = q_ref[pl.ds(r0, WINDOW), c0:c0 + LANES]
                qs = jnp.concatenate([jnp.where(lo1, q2, zero), jnp.where(lo1, zero, q2)], axis=0)
                s = _dot_nt(qs, kk) * (HEAD_DIM ** -0.5)
                probs = []
                for hh in range(2):
                    h = g * Q_PER_KV + 2 * pair + hh
                    slope = 2.0 ** (-8.0 * (h + 1) / ATTN_HEADS)
                    sh = s[hh * WINDOW:(hh + 1) * WINDOW] - slope * distf
                    sh = jnp.where(valid, sh, -1e30)
                    sink = sink_ref[h]
                    m = jnp.maximum(jnp.max(sh, axis=-1, keepdims=True), sink)
                    e = jnp.exp(sh - m)
                    den = jnp.sum(e, axis=-1, keepdims=True) + jnp.exp(sink - m)
                    probs.append((e * (1.0 / den)).astype(BF16))
                outs.append(_dot(probs[0], vlo) + _dot(probs[1], vhi))
            o_ref[pl.ds(r0, WINDOW), g * 2 * LANES:(g + 1) * 2 * LANES] = (
                jnp.concatenate(outs, axis=1).astype(BF16))
        return carry

    lax.fori_loop(0, TQ_ATTN // WINDOW, block, 0)


def _attention(proj, sinks, batch, seq):
    t = proj.shape[0]
    tiles = seq // TQ_ATTN
    blocks_per_tile = TQ_ATTN // WINDOW
    kd_blk = COL_KD // (2 * KV_WIDTH)
    vd_blk = COL_VD // (2 * KV_WIDTH)

    def cur(col):
        return lambda b, i: (b * tiles + i, col)

    def prev(col):
        return lambda b, i: (jnp.maximum((b * tiles + i) * blocks_per_tile - 1, 0), col)

    return pl.pallas_call(
        _attn_kernel,
        grid=(batch, tiles),
        in_specs=[
            pl.BlockSpec(memory_space=pltpu.SMEM),
            pl.BlockSpec((TQ_ATTN, ATTN_WIDTH), cur(COL_Q // ATTN_WIDTH)),
            pl.BlockSpec((TQ_ATTN, 2 * KV_WIDTH), cur(kd_blk)),
            pl.BlockSpec((WINDOW, 2 * KV_WIDTH), prev(kd_blk)),
            pl.BlockSpec((TQ_ATTN, 2 * KV_WIDTH), cur(vd_blk)),
            pl.BlockSpec((WINDOW, 2 * KV_WIDTH), prev(vd_blk)),
        ],
        out_specs=pl.BlockSpec((TQ_ATTN, ATTN_WIDTH), lambda b, i: (b * tiles + i, 0)),
        out_shape=jax.ShapeDtypeStruct((t, ATTN_WIDTH), BF16),
        scratch_shapes=[pltpu.VMEM((TQ_ATTN + WINDOW, 2 * KV_WIDTH), BF16)] * 3,
        compiler_params=pltpu.CompilerParams(
            dimension_semantics=("arbitrary", "arbitrary"), vmem_limit_bytes=VMEM_LIMIT),
        name="swa_attention",
    )(sinks, proj, proj, proj, proj, proj)


def _ssd_kernel(z_ref, xs_ref, b_ref, c_ref, dt_ref, cw_ref, cb_ref, dtb_ref, alog_ref, dskip_ref,
                snorm_ref, e_ref, o_ref, halo_ref, state_ref, xc_ref):
    chunk = pl.program_id(1)

    @pl.when(chunk == 0)
    def _():
        halo_ref[...] = jnp.zeros_like(halo_ref)
        state_ref[...] = jnp.zeros_like(state_ref)

    def conv(raw_ref, col0, width):
        for c0 in range(0, width, GROUP_WIDTH):
            cols = slice(col0 + c0, col0 + c0 + GROUP_WIDTH)
            x = raw_ref[:, c0:c0 + GROUP_WIDTH].astype(F32)
            ext = jnp.concatenate([halo_ref[:, cols], x], axis=0)
            acc = x * cw_ref[CONV_WIDTH - 1:CONV_WIDTH, cols] + cb_ref[:, cols]
            for s in range(1, CONV_WIDTH):
                shifted = pltpu.roll(ext, s, axis=0)[HALO_ROWS:]
                acc = acc + shifted * cw_ref[CONV_WIDTH - 1 - s:CONV_WIDTH - s, cols]
            halo_ref[:, cols] = x[CHUNK - HALO_ROWS:]
            xc_ref[:, cols] = _silu(acc)

    conv(xs_ref, 0, D_INNER)
    conv(b_ref, D_INNER, BC_WIDTH)
    conv(c_ref, D_INNER + BC_WIDTH, BC_WIDTH)

    head_lane = lax.broadcasted_iota(jnp.int32, (1, LANES), 1) < SSM_HEADS
    pre = dt_ref[...] + dtb_ref[...]
    dt = jnp.maximum(pre, 0.0) + jnp.log(1.0 + jnp.exp(-jnp.abs(pre)))
    dta = jnp.where(head_lane, dt * (-jnp.exp(alog_ref[...])), 0.0)
    ri = lax.broadcasted_iota(jnp.int32, (CHUNK, CHUNK), 0)
    ci = lax.broadcasted_iota(jnp.int32, (CHUNK, CHUNK), 1)
    causal = ri >= ci
    acs = _dot_exact_lhs(causal.astype(BF16), dta)
    acs_t = acs.T
    acs_last = acs[CHUNK - 1:CHUNK, :]
    stack = jnp.concatenate([
        dt,
        dt * jnp.exp(acs_last - acs),
        jnp.exp(acs),
        jnp.broadcast_to(jnp.exp(acs_last), (HALO_ROWS, LANES)),
        jnp.broadcast_to(dskip_ref[...], (HALO_ROWS, LANES)),
    ], axis=0)
    lo1 = lax.broadcasted_iota(jnp.int32, (1, LANES), 1) < SSM_HEAD_DIM

    for g in range(SSM_GROUPS):
        gcols = slice(g * GROUP_WIDTH, (g + 1) * GROUP_WIDTH)
        xg = xc_ref[:, gcols]
        bg = xc_ref[:, D_INNER + g * D_STATE:D_INNER + (g + 1) * D_STATE]
        cg = xc_ref[:, D_INNER + BC_WIDTH + g * D_STATE:D_INNER + BC_WIDTH + (g + 1) * D_STATE]
        bb = bg.astype(BF16)
        cbf = cg.astype(BF16)
        ex = _dot_exact_rhs(stack, e_ref[:, gcols])
        dt_e = ex[0:CHUNK]
        dtdec_e = ex[CHUNK:2 * CHUNK]
        eacs_e = ex[2 * CHUNK:3 * CHUNK]
        cdec_e = ex[3 * CHUNK:3 * CHUNK + 1]
        dskip_e = ex[3 * CHUNK + HALO_ROWS:3 * CHUNK + HALO_ROWS + 1]
        xdt = xg * dt_e

        cb = _dot_nt(cbf, bb)
        st = state_ref[:, gcols]
        y_off = _dot(cbf, st.astype(BF16)) * eacs_e
        new_state = _dot(bg.T.astype(BF16), (xg * dtdec_e).astype(BF16))
        state_ref[:, gcols] = st * cdec_e + new_state

        pairs = []
        for p in range(HEADS_PER_GROUP // 2):
            pcols = slice(p * LANES, (p + 1) * LANES)
            ms = []
            for hh in range(2):
                h = g * HEADS_PER_GROUP + 2 * p + hh
                diff = acs[:, h:h + 1] - acs_t[h:h + 1, :]
                decay = jnp.exp(jnp.where(causal, diff, -1e30))
                ms.append((cb * decay).astype(BF16))
            xp = xdt[:, pcols]
            rhs = jnp.concatenate([jnp.where(lo1, xp, 0.0), jnp.where(lo1, 0.0, xp)], axis=0)
            y_diag = _dot(jnp.concatenate(ms, axis=1), rhs.astype(BF16))
            pairs.append(y_diag + y_off[:, pcols] + dskip_e[:, pcols] * xg[:, pcols])
        y = jnp.concatenate(pairs, axis=1)
        y = y * _silu(z_ref[:, gcols].astype(F32))
        o_ref[:, gcols] = (_rms_normalize(y) * snorm_ref[:, gcols]).astype(BF16)


def _ssd(proj, dt_raw, conv_w, conv_b, dt_bias, a_log, d_skip, ssm_norm, expand, batch, seq):
    t = proj.shape[0]
    chunks = seq // CHUNK

    def rows(col):
        return lambda b, c: (b * chunks + c, col)

    def const(b, c):
        return (0, 0)

    return pl.pallas_call(
        _ssd_kernel,
        grid=(batch, chunks),
        in_specs=[
            pl.BlockSpec((CHUNK, D_INNER), rows(COL_Z // D_INNER)),
            pl.BlockSpec((CHUNK, D_INNER), rows(COL_XS // D_INNER)),
            pl.BlockSpec((CHUNK, BC_WIDTH), rows(COL_B // BC_WIDTH)),
            pl.BlockSpec((CHUNK, BC_WIDTH), rows(COL_C // BC_WIDTH)),
            pl.BlockSpec((CHUNK, LANES), rows(0)),
            pl.BlockSpec((CONV_WIDTH, CONV_CH), const),
            pl.BlockSpec((1, CONV_CH), const),
            pl.BlockSpec((1, LANES), const),
            pl.BlockSpec((1, LANES), const),
            pl.BlockSpec((1, LANES), const),
            pl.BlockSpec((1, D_INNER), const),
            pl.BlockSpec((LANES, D_INNER), const),
        ],
        out_specs=pl.BlockSpec((CHUNK, D_INNER), lambda b, c: (b * chunks + c, 0)),
        out_shape=jax.ShapeDtypeStruct((t, D_INNER), BF16),
        scratch_shapes=[
            pltpu.VMEM((HALO_ROWS, CONV_CH), F32),
            pltpu.VMEM((D_STATE, D_INNER), F32),
            pltpu.VMEM((CHUNK, CONV_CH), F32),
        ],
        compiler_params=pltpu.CompilerParams(
            dimension_semantics=("arbitrary", "arbitrary"), vmem_limit_bytes=VMEM_LIMIT),
        name="ssd",
    )(proj, proj, proj, proj, dt_raw, conv_w, conv_b, dt_bias, a_log, d_skip, ssm_norm, expand)


def _merge_kernel(x_ref, mod_ref, npost_ref, ao_ref, ys_ref, ga_ref, gs_ref, wa_ref, ws_ref, wo_ref,
                  o_ref):
    y_a = _dot(ao_ref[...], wa_ref[...])
    y_s = _dot(ys_ref[...], ws_ref[...])
    merged = (_sigmoid(ga_ref[...].astype(F32)) * y_a + _sigmoid(gs_ref[...].astype(F32)) * y_s)
    t = _dot(merged.astype(BF16), wo_ref[...])
    r = _rms_normalize(t) * npost_ref[...]
    o_ref[...] = x_ref[...] + mod_ref[0, 5:6, :] * r


def _merge(x, mod_l, npost, ao, ys, proj, wa, ws, wo, layer, seq):
    t = x.shape[0]
    tiles_per_seq = seq // TM_MERGE
    resident = pl.Buffered(1)
    return pl.pallas_call(
        _merge_kernel,
        grid=(t // TM_MERGE,),
        in_specs=[
            pl.BlockSpec((TM_MERGE, D_MODEL), lambda i: (i, 0)),
            pl.BlockSpec((1, N_MOD, D_MODEL), lambda i: (i // tiles_per_seq, 0, 0)),
            pl.BlockSpec((1, D_MODEL), lambda i: (0, 0)),
            pl.BlockSpec((TM_MERGE, ATTN_WIDTH), lambda i: (i, 0)),
            pl.BlockSpec((TM_MERGE, D_INNER), lambda i: (i, 0)),
            pl.BlockSpec((TM_MERGE, D_MODEL), lambda i: (i, COL_GA // D_MODEL)),
            pl.BlockSpec((TM_MERGE, D_MODEL), lambda i: (i, COL_GS // D_MODEL)),
            pl.BlockSpec((None, ATTN_WIDTH, D_MODEL), lambda i: (layer, 0, 0), pipeline_mode=resident),
            pl.BlockSpec((None, D_INNER, D_MODEL), lambda i: (layer, 0, 0), pipeline_mode=resident),
            pl.BlockSpec((None, D_MODEL, D_MODEL), lambda i: (layer, 0, 0), pipeline_mode=resident),
        ],
        out_specs=pl.BlockSpec((TM_MERGE, D_MODEL), lambda i: (i, 0)),
        out_shape=jax.ShapeDtypeStruct((t, D_MODEL), F32),
        compiler_params=pltpu.CompilerParams(
            dimension_semantics=("arbitrary",), vmem_limit_bytes=VMEM_LIMIT),
        name="merge_out",
    )(x, mod_l, npost, ao, ys, proj, proj, wa, ws, wo)


def _reorder_w_in(w_in):
    o = 0
    q = w_in[..., o:o + ATTN_WIDTH]; o += ATTN_WIDTH
    k = w_in[..., o:o + KV_WIDTH]; o += KV_WIDTH
    v = w_in[..., o:o + KV_WIDTH]; o += KV_WIDTH
    z = w_in[..., o:o + D_INNER]; o += D_INNER
    xs = w_in[..., o:o + D_INNER]; o += D_INNER
    bm = w_in[..., o:o + BC_WIDTH]; o += BC_WIDTH
    cm = w_in[..., o:o + BC_WIDTH]; o += BC_WIDTH
    dt = w_in[..., o:o + SSM_HEADS]; o += SSM_HEADS
    ga = w_in[..., o:o + D_MODEL]; o += D_MODEL
    gs = w_in[..., o:o + D_MODEL]

    def dup(w):
        w4 = w.reshape(DEPTH, D_MODEL, ATTN_KV_HEADS, HEAD_DIM)
        return jnp.concatenate([w4, w4], axis=-1).reshape(DEPTH, D_MODEL, 2 * KV_WIDTH)

    main = jnp.concatenate([z, ga, gs, xs, q, bm, cm, dup(k), dup(v)], axis=-1).astype(BF16)
    dt_w = jnp.pad(dt, ((0, 0), (0, 0), (0, LANES - SSM_HEADS))).astype(BF16)
    return main, dt_w


def _pad_heads(p):
    return jnp.pad(p, ((0, 0), (0, LANES - SSM_HEADS))).reshape(DEPTH, 1, LANES)


def kernel(x, c, w_mod, b_mod, norm_pre, norm_post, w_ffn_gate, w_ffn_up, w_ffn_down, w_in,
           attn_sinks, conv_w, conv_b, dt_bias, a_log, d_skip, ssm_norm, w_attn_out, w_ssm_out, w_out):
    batch, seq, _ = x.shape
    assert seq % max(TM_FFN, TM_PROJ, TQ_ATTN, TM_MERGE, CHUNK) == 0
    assert batch <= HALO_ROWS
    t = batch * seq

    c_pad = jnp.pad(c, ((0, HALO_ROWS - batch), (0, 0)))
    mod = _mod_all(c_pad, w_mod, b_mod).reshape(DEPTH, HALO_ROWS, N_MOD, D_MODEL)[:, :batch]

    wg = w_ffn_gate.astype(BF16)
    wu = w_ffn_up.astype(BF16)
    wd = w_ffn_down.astype(BF16)
    w_main, w_dt = _reorder_w_in(w_in)
    wa = w_attn_out.astype(BF16)
    ws = w_ssm_out.astype(BF16)
    wo = w_out.astype(BF16)
    dtb = _pad_heads(dt_bias)
    alog = _pad_heads(a_log)
    dsk = _pad_heads(d_skip)
    lane_head = lax.broadcasted_iota(jnp.int32, (LANES, D_INNER), 1) // SSM_HEAD_DIM
    expand = (lane_head == lax.broadcasted_iota(jnp.int32, (LANES, D_INNER), 0)).astype(BF16)

    xf = x.reshape(t, D_MODEL)
    for l in range(DEPTH):
        npre = norm_pre[l].reshape(3, 1, D_MODEL)
        npost = norm_post[l].reshape(3, 1, D_MODEL)
        xf = _ffn(xf, mod[l], npre[0], npost[0], wg, wu, wd, l, 0, seq)
        proj, dt_raw = _inproj(xf, mod[l], npre[1], w_main, w_dt, l, seq)
        ao = _attention(proj, attn_sinks[l], batch, seq)
        ys = _ssd(proj, dt_raw, conv_w[l], conv_b[l].reshape(1, CONV_CH), dtb[l], alog[l], dsk[l],
                  ssm_norm[l].reshape(1, D_INNER), expand, batch, seq)
        xf = _merge(xf, mod[l], npost[1], ao, ys, proj, wa, ws, wo, l, seq)
        xf = _ffn(xf, mod[l], npre[2], npost[2], wg, wu, wd, l, 1, seq)
    return xf.reshape(batch, seq, D_MODEL)
```

```python
import functools

import jax
import jax.numpy as jnp
from jax import lax
from jax.experimental import pallas as pl
from jax.experimental.pallas import tpu as pltpu

F32 = jnp.float32
BF16 = jnp.bfloat16

D_MODEL = 2048
DEPTH = 4
ATTN_HEADS = 16
ATTN_KV_HEADS = 4
HEAD_DIM = 64
WINDOW = 128
ATTN_WIDTH = ATTN_HEADS * HEAD_DIM
KV_WIDTH = ATTN_KV_HEADS * HEAD_DIM
D_INNER = D_MODEL
SSM_HEAD_DIM = 64
SSM_HEADS = D_INNER // SSM_HEAD_DIM
SSM_GROUPS = 4
D_STATE = 128
CONV_WIDTH = 4
CHUNK = 128
BC_WIDTH = SSM_GROUPS * D_STATE
CONV_CH = D_INNER + 2 * BC_WIDTH
D_FF = 5632
N_MOD = 9
NORM_EPS = 1e-6
GROUP_WIDTH = D_INNER // SSM_GROUPS
HEADS_PER_GROUP = SSM_HEADS // SSM_GROUPS
Q_PER_KV = ATTN_HEADS // ATTN_KV_HEADS

LANES = 128
HALO_ROWS = 8
VMEM_LIMIT = 56 * 1024 * 1024

COL_Z = 0
COL_GA = COL_Z + D_INNER
COL_GS = COL_GA + D_MODEL
COL_XS = COL_GS + D_MODEL
COL_Q = COL_XS + D_INNER
COL_B = COL_Q + ATTN_WIDTH
COL_C = COL_B + BC_WIDTH
COL_KD = COL_C + BC_WIDTH
COL_VD = COL_KD + 2 * KV_WIDTH
P_MAIN = COL_VD + 2 * KV_WIDTH

TM_FFN = 1024
TR_FFN = 256
TF_FFN = 512
TM_PROJ = 1024
TN_PROJ = 1024
TQ_ATTN = 512
TM_MERGE = 512
TR_MERGE = 256
TM_CAST = 256
TN_MOD = 1024


def _dot(a, b):
    return jnp.dot(a, b, preferred_element_type=F32)


def _dot_nt(a, b):
    return lax.dot_general(a, b, (((1,), (1,)), ((), ())), preferred_element_type=F32)


def _sigmoid(x):
    return 1.0 / (1.0 + jnp.exp(-x))


def _silu(x):
    return x * _sigmoid(x)


def _rms_normalize(x):
    return x * lax.rsqrt(jnp.mean(x * x, axis=-1, keepdims=True) + NORM_EPS)


def _split3(v):
    hi = v.astype(BF16)
    r1 = v - hi.astype(F32)
    mid = r1.astype(BF16)
    lo = (r1 - mid.astype(F32)).astype(BF16)
    return hi, mid, lo


def _dot_exact_rhs(v, m):
    hi, mid, lo = _split3(v)
    return _dot(hi, m) + _dot(mid, m) + _dot(lo, m)


def _dot_exact_lhs(m, v):
    hi, mid, lo = _split3(v)
    return _dot(m, hi) + _dot(m, mid) + _dot(m, lo)


def _cast_kernel(w_ref, o_ref):
    o_ref[...] = w_ref[...].astype(BF16)


def _cast_bf16(w):
    cols = w.shape[-1]
    w2 = w.reshape(-1, cols)
    out = pl.pallas_call(
        _cast_kernel,
        grid=(w2.shape[0] // TM_CAST,),
        in_specs=[pl.BlockSpec((TM_CAST, cols), lambda i: (i, 0))],
        out_specs=pl.BlockSpec((TM_CAST, cols), lambda i: (i, 0)),
        out_shape=jax.ShapeDtypeStruct(w2.shape, BF16),
        compiler_params=pltpu.CompilerParams(
            dimension_semantics=("arbitrary",), vmem_limit_bytes=VMEM_LIMIT),
        name="cast_bf16",
    )(w2)
    return out.reshape(w.shape)


def _mod_kernel(c_ref, w_ref, b_ref, o_ref):
    c = c_ref[...]
    o_ref[0] = _dot(_silu(c).astype(BF16), w_ref[0].astype(BF16)) + b_ref[0]


def _mod_all(c_pad, w_mod, b_mod):
    rows = c_pad.shape[0]
    n = N_MOD * D_MODEL
    return pl.pallas_call(
        _mod_kernel,
        grid=(DEPTH, n // TN_MOD),
        in_specs=[
            pl.BlockSpec((rows, D_MODEL), lambda l, j: (0, 0)),
            pl.BlockSpec((1, D_MODEL, TN_MOD), lambda l, j: (l, 0, j)),
            pl.BlockSpec((1, 1, TN_MOD), lambda l, j: (l, 0, j)),
        ],
        out_specs=pl.BlockSpec((1, rows, TN_MOD), lambda l, j: (l, 0, j)),
        out_shape=jax.ShapeDtypeStruct((DEPTH, rows, n), F32),
        compiler_params=pltpu.CompilerParams(
            dimension_semantics=("arbitrary", "arbitrary"), vmem_limit_bytes=VMEM_LIMIT),
        name="mod_proj",
    )(c_pad, w_mod, b_mod.reshape(DEPTH, 1, n))


def _ffn_kernel(x_ref, mod_ref, npre_ref, npost_ref, wg_ref, wu_ref, wd_ref, o_ref, h_ref, *, mod_base):
    j = pl.program_id(1)
    row_chunks = [slice(r, r + TR_FFN) for r in range(0, TM_FFN, TR_FFN)]

    @pl.when(j == 0)
    def _():
        w = npre_ref[...] * (1.0 + mod_ref[0, mod_base + 1:mod_base + 2, :])
        shift = mod_ref[0, mod_base:mod_base + 1, :]
        for rows in row_chunks:
            h_ref[rows, :] = (_rms_normalize(x_ref[rows, :]) * w + shift).astype(BF16)
        o_ref[...] = jnp.zeros_like(o_ref)

    for rows in row_chunks:
        h = h_ref[rows, :]
        gate = _dot(h, wg_ref[...])
        up = _dot(h, wu_ref[...])
        o_ref[rows, :] += _dot((_silu(gate) * up).astype(BF16), wd_ref[...])

    @pl.when(j == pl.num_programs(1) - 1)
    def _():
        w = npost_ref[...] * (0.5 * mod_ref[0, mod_base + 2:mod_base + 3, :])
        for rows in row_chunks:
            o_ref[rows, :] = x_ref[rows, :] + _rms_normalize(o_ref[rows, :]) * w


def _ffn(x, mod_l, npre, npost, wg, wu, wd, layer, which, seq):
    t = x.shape[0]
    tiles_per_seq = seq // TM_FFN
    mod_base = 0 if which == 0 else 6
    kern = functools.partial(_ffn_kernel, mod_base=mod_base)
    return pl.pallas_call(
        kern,
        grid=(t // TM_FFN, D_FF // TF_FFN),
        in_specs=[
            pl.BlockSpec((TM_FFN, D_MODEL), lambda i, j: (i, 0)),
            pl.BlockSpec((1, N_MOD, D_MODEL), lambda i, j: (i // tiles_per_seq, 0, 0)),
            pl.BlockSpec((1, D_MODEL), lambda i, j: (0, 0)),
            pl.BlockSpec((1, D_MODEL), lambda i, j: (0, 0)),
            pl.BlockSpec((None, None, D_MODEL, TF_FFN), lambda i, j: (layer, which, 0, j)),
            pl.BlockSpec((None, None, D_MODEL, TF_FFN), lambda i, j: (layer, which, 0, j)),
            pl.BlockSpec((None, None, TF_FFN, D_MODEL), lambda i, j: (layer, which, j, 0)),
        ],
        out_specs=pl.BlockSpec((TM_FFN, D_MODEL), lambda i, j: (i, 0)),
        out_shape=jax.ShapeDtypeStruct((t, D_MODEL), F32),
        scratch_shapes=[pltpu.VMEM((TM_FFN, D_MODEL), BF16)],
        compiler_params=pltpu.CompilerParams(
            dimension_semantics=("arbitrary", "arbitrary"), vmem_limit_bytes=VMEM_LIMIT),
        name="ffn",
    )(x, mod_l, npre, npost, wg, wu, wd)


def _inproj_kernel(x_ref, mod_ref, npre_ref, w_ref, wdt_ref, o_ref, dt_ref, h_ref):
    j = pl.program_id(1)

    @pl.when(j == 0)
    def _():
        w = npre_ref[...] * (1.0 + mod_ref[0, 4:5, :])
        shift = mod_ref[0, 3:4, :]
        for r0 in range(0, TM_PROJ, TR_FFN):
            rows = slice(r0, r0 + TR_FFN)
            h = (_rms_normalize(x_ref[rows, :]) * w + shift).astype(BF16)
            h_ref[rows, :] = h
            dt_ref[rows, :] = _dot(h, wdt_ref[...])

    o_ref[...] = _dot(h_ref[...], w_ref[...]).astype(BF16)


def _inproj(x, mod_l, npre, w_main, w_dt, layer, seq):
    t = x.shape[0]
    tiles_per_seq = seq // TM_PROJ
    return pl.pallas_call(
        _inproj_kernel,
        grid=(t // TM_PROJ, P_MAIN // TN_PROJ),
        in_specs=[
            pl.BlockSpec((TM_PROJ, D_MODEL), lambda i, j: (i, 0)),
            pl.BlockSpec((1, N_MOD, D_MODEL), lambda i, j: (i // tiles_per_seq, 0, 0)),
            pl.BlockSpec((1, D_MODEL), lambda i, j: (0, 0)),
            pl.BlockSpec((None, D_MODEL, TN_PROJ), lambda i, j: (layer, 0, j)),
            pl.BlockSpec((None, D_MODEL, LANES), lambda i, j: (layer, 0, 0)),
        ],
        out_specs=[
            pl.BlockSpec((TM_PROJ, TN_PROJ), lambda i, j: (i, j)),
            pl.BlockSpec((TM_PROJ, LANES), lambda i, j: (i, 0)),
        ],
        out_shape=[jax.ShapeDtypeStruct((t, P_MAIN), BF16), jax.ShapeDtypeStruct((t, LANES), F32)],
        scratch_shapes=[pltpu.VMEM((TM_PROJ, D_MODEL), BF16)],
        compiler_params=pltpu.CompilerParams(
            dimension_semantics=("arbitrary", "arbitrary"), vmem_limit_bytes=VMEM_LIMIT),
        name="in_proj",
    )(x, mod_l, npre, w_main, w_dt)


def _attn_kernel(sink_ref, q_ref, kc_ref, kp_ref, vc_ref, vp_ref, o_ref, kk_ref, vlo_ref, vhi_ref):
    i = pl.program_id(1)
    lo = lax.broadcasted_iota(jnp.int32, (1, 2 * KV_WIDTH), 1) % LANES < HEAD_DIM
    zero = jnp.zeros((), BF16)
    kk_ref[0:WINDOW, :] = kp_ref[...]
    kk_ref[WINDOW:, :] = kc_ref[...]
    vp = vp_ref[...]
    vc = vc_ref[...]
    vlo_ref[0:WINDOW, :] = jnp.where(lo, vp, zero)
    vlo_ref[WINDOW:, :] = jnp.where(lo, vc, zero)
    vhi_ref[0:WINDOW, :] = jnp.where(lo, zero, vp)
    vhi_ref[WINDOW:, :] = jnp.where(lo, zero, vc)

    row = lax.broadcasted_iota(jnp.int32, (WINDOW, 2 * WINDOW), 0)
    col = lax.broadcasted_iota(jnp.int32, (WINDOW, 2 * WINDOW), 1)
    dist = WINDOW + row - col
    band = (dist >= 0) & (dist < WINDOW)
    distf = dist.astype(F32)
    lo1 = lax.broadcasted_iota(jnp.int32, (1, LANES), 1) < HEAD_DIM

    def block(n, carry):
        r0 = pl.multiple_of(n * WINDOW, WINDOW)
        first_key = jnp.where((i == 0) & (n == 0), WINDOW, 0)
        valid = band & (col >= first_key)
        for g in range(ATTN_KV_HEADS):
            kk = kk_ref[pl.ds(r0, 2 * WINDOW), g * LANES:(g + 1) * LANES]
            vlo = vlo_ref[pl.ds(r0, 2 * WINDOW), g * LANES:(g + 1) * LANES]
            vhi = vhi_ref[pl.ds(r0, 2 * WINDOW), g * LANES:(g + 1) * LANES]
            outs = []
            for pair in range(Q_PER_KV // 2):
                c0 = (g * Q_PER_KV + 2 * pair) * HEAD_DIM
                q2 = q_ref[pl.ds(r0, WINDOW), c0:c0 + LANES]
                qs = jnp.concatenate([jnp.where(lo1, q2, zero), jnp.where(lo1, zero, q2)], axis=0)
                s = _dot_nt(qs, kk) * (HEAD_DIM ** -0.5)
                probs = []
                for hh in range(2):
                    h = g * Q_PER_KV + 2 * pair + hh
                    slope = 2.0 ** (-8.0 * (h + 1) / ATTN_HEADS)
                    sh = s[hh * WINDOW:(hh + 1) * WINDOW] - slope * distf
                    sh = jnp.where(valid, sh, -1e30)
                    sink = sink_ref[h]
                    m = jnp.maximum(jnp.max(sh, axis=-1, keepdims=True), sink)
                    e = jnp.exp(sh - m)
                    den = jnp.sum(e, axis=-1, keepdims=True) + jnp.exp(sink - m)
                    probs.append((e * (1.0 / den)).astype(BF16))
                outs.append(_dot(probs[0], vlo) + _dot(probs[1], vhi))
            o_ref[pl.ds(r0, WINDOW), g * 2 * LANES:(g + 1) * 2 * LANES] = (
                jnp.concatenate(outs, axis=1).astype(BF16))
        return carry

    lax.fori_loop(0, TQ_ATTN // WINDOW, block, 0)


def _attention(proj, sinks, batch, seq):
    t = proj.shape[0]
    tiles = seq // TQ_ATTN
    blocks_per_tile = TQ_ATTN // WINDOW
    kd_blk = COL_KD // (2 * KV_WIDTH)
    vd_blk = COL_VD // (2 * KV_WIDTH)

    def cur(col):
        return lambda b, i: (b * tiles + i, col)

    def prev(col):
        return lambda b, i: (jnp.maximum((b * tiles + i) * blocks_per_tile - 1, 0), col)

    return pl.pallas_call(
        _attn_kernel,
        grid=(batch, tiles),
        in_specs=[
            pl.BlockSpec(memory_space=pltpu.SMEM),
            pl.BlockSpec((TQ_ATTN, ATTN_WIDTH), cur(COL_Q // ATTN_WIDTH)),
            pl.BlockSpec((TQ_ATTN, 2 * KV_WIDTH), cur(kd_blk)),
            pl.BlockSpec((WINDOW, 2 * KV_WIDTH), prev(kd_blk)),
            pl.BlockSpec((TQ_ATTN, 2 * KV_WIDTH), cur(vd_blk)),
            pl.BlockSpec((WINDOW, 2 * KV_WIDTH), prev(vd_blk)),
        ],
        out_specs=pl.BlockSpec((TQ_ATTN, ATTN_WIDTH), lambda b, i: (b * tiles + i, 0)),
        out_shape=jax.ShapeDtypeStruct((t, ATTN_WIDTH), BF16),
        scratch_shapes=[pltpu.VMEM((TQ_ATTN + WINDOW, 2 * KV_WIDTH), BF16)] * 3,
        compiler_params=pltpu.CompilerParams(
            dimension_semantics=("arbitrary", "arbitrary"), vmem_limit_bytes=VMEM_LIMIT),
        name="swa_attention",
    )(sinks, proj, proj, proj, proj, proj)


def _ssd_kernel(z_ref, xs_ref, b_ref, c_ref, dt_ref, cw_ref, cb_ref, dtb_ref, alog_ref, dskip_ref,
                snorm_ref, e_ref, o_ref, halo_ref, state_ref, xc_ref):
    chunk = pl.program_id(1)

    @pl.when(chunk == 0)
    def _():
        halo_ref[...] = jnp.zeros_like(halo_ref)
        state_ref[...] = jnp.zeros_like(state_ref)

    def conv(raw_ref, col0, width):
        for c0 in range(0, width, GROUP_WIDTH):
            cols = slice(col0 + c0, col0 + c0 + GROUP_WIDTH)
            x = raw_ref[:, c0:c0 + GROUP_WIDTH].astype(F32)
            ext = jnp.concatenate([halo_ref[:, cols], x], axis=0)
            acc = x * cw_ref[CONV_WIDTH - 1:CONV_WIDTH, cols] + cb_ref[:, cols]
            for s in range(1, CONV_WIDTH):
                shifted = pltpu.roll(ext, s, axis=0)[HALO_ROWS:]
                acc = acc + shifted * cw_ref[CONV_WIDTH - 1 - s:CONV_WIDTH - s, cols]
            halo_ref[:, cols] = x[CHUNK - HALO_ROWS:]
            xc_ref[:, cols] = _silu(acc)

    conv(xs_ref, 0, D_INNER)
    conv(b_ref, D_INNER, BC_WIDTH)
    conv(c_ref, D_INNER + BC_WIDTH, BC_WIDTH)

    head_lane = lax.broadcasted_iota(jnp.int32, (1, LANES), 1) < SSM_HEADS
    pre = dt_ref[...] + dtb_ref[...]
    dt = jnp.maximum(pre, 0.0) + jnp.log(1.0 + jnp.exp(-jnp.abs(pre)))
    dta = jnp.where(head_lane, dt * (-jnp.exp(alog_ref[...])), 0.0)
    ri = lax.broadcasted_iota(jnp.int32, (CHUNK, CHUNK), 0)
    ci = lax.broadcasted_iota(jnp.int32, (CHUNK, CHUNK), 1)
    causal = ri >= ci
    acs = _dot_exact_lhs(causal.astype(BF16), dta)
    acs_t = acs.T
    acs_last = acs[CHUNK - 1:CHUNK, :]
    stack = jnp.concatenate([
        dt,
        dt * jnp.exp(acs_last - acs),
        jnp.exp(acs),
        jnp.broadcast_to(jnp.exp(acs_last), (HALO_ROWS, LANES)),
        jnp.broadcast_to(dskip_ref[...], (HALO_ROWS, LANES)),
    ], axis=0)
    lo1 = lax.broadcasted_iota(jnp.int32, (1, LANES), 1) < SSM_HEAD_DIM

    for g in range(SSM_GROUPS):
        gcols = slice(g * GROUP_WIDTH, (g + 1) * GROUP_WIDTH)
        xg = xc_ref[:, gcols]
        bg = xc_ref[:, D_INNER + g * D_STATE:D_INNER + (g + 1) * D_STATE]
        cg = xc_ref[:, D_INNER + BC_WIDTH + g * D_STATE:D_INNER + BC_WIDTH + (g + 1) * D_STATE]
        bb = bg.astype(BF16)
        cbf = cg.astype(BF16)
        ex = _dot_exact_rhs(stack, e_ref[:, gcols])
        dt_e = ex[0:CHUNK]
        dtdec_e = ex[CHUNK:2 * CHUNK]
        eacs_e = ex[2 * CHUNK:3 * CHUNK]
        cdec_e = ex[3 * CHUNK:3 * CHUNK + 1]
        dskip_e = ex[3 * CHUNK + HALO_ROWS:3 * CHUNK + HALO_ROWS + 1]
        xdt = xg * dt_e

        cb = _dot_nt(cbf, bb)
        st = state_ref[:, gcols]
        y_off = _dot(cbf, st.astype(BF16)) * eacs_e
        new_state = _dot(bg.T.astype(BF16), (xg * dtdec_e).astype(BF16))
        state_ref[:, gcols] = st * cdec_e + new_state

        pairs = []
        for p in range(HEADS_PER_GROUP // 2):
            pcols = slice(p * LANES, (p + 1) * LANES)
            ms = []
            for hh in range(2):
                h = g * HEADS_PER_GROUP + 2 * p + hh
                diff = acs[:, h:h + 1] - acs_t[h:h + 1, :]
                decay = jnp.exp(jnp.where(causal, diff, -1e30))
                ms.append((cb * decay).astype(BF16))
            xp = xdt[:, pcols]
            rhs = jnp.concatenate([jnp.where(lo1, xp, 0.0), jnp.where(lo1, 0.0, xp)], axis=0)
            y_diag = _dot(jnp.concatenate(ms, axis=1), rhs.astype(BF16))
            pairs.append(y_diag + y_off[:, pcols] + dskip_e[:, pcols] * xg[:, pcols])
        y = jnp.concatenate(pairs, axis=1)
        y = y * _silu(z_ref[:, gcols].astype(F32))
        o_ref[:, gcols] = (_rms_normalize(y) * snorm_ref[:, gcols]).astype(BF16)


def _ssd(proj, dt_raw, conv_w, conv_b, dt_bias, a_log, d_skip, ssm_norm, expand, batch, seq):
    t = proj.shape[0]
    chunks = seq // CHUNK

    def rows(col):
        return lambda b, c: (b * chunks + c, col)

    def const(b, c):
        return (0, 0)

    return pl.pallas_call(
        _ssd_kernel,
        grid=(batch, chunks),
        in_specs=[
            pl.BlockSpec((CHUNK, D_INNER), rows(COL_Z // D_INNER)),
            pl.BlockSpec((CHUNK, D_INNER), rows(COL_XS // D_INNER)),
            pl.BlockSpec((CHUNK, BC_WIDTH), rows(COL_B // BC_WIDTH)),
            pl.BlockSpec((CHUNK, BC_WIDTH), rows(COL_C // BC_WIDTH)),
            pl.BlockSpec((CHUNK, LANES), rows(0)),
            pl.BlockSpec((CONV_WIDTH, CONV_CH), const),
            pl.BlockSpec((1, CONV_CH), const),
            pl.BlockSpec((1, LANES), const),
            pl.BlockSpec((1, LANES), const),
            pl.BlockSpec((1, LANES), const),
            pl.BlockSpec((1, D_INNER), const),
            pl.BlockSpec((LANES, D_INNER), const),
        ],
        out_specs=pl.BlockSpec((CHUNK, D_INNER), lambda b, c: (b * chunks + c, 0)),
        out_shape=jax.ShapeDtypeStruct((t, D_INNER), BF16),
        scratch_shapes=[
            pltpu.VMEM((HALO_ROWS, CONV_CH), F32),
            pltpu.VMEM((D_STATE, D_INNER), F32),
            pltpu.VMEM((CHUNK, CONV_CH), F32),
        ],
        compiler_params=pltpu.CompilerParams(
            dimension_semantics=("arbitrary", "arbitrary"), vmem_limit_bytes=VMEM_LIMIT),
        name="ssd",
    )(proj, proj, proj, proj, dt_raw, conv_w, conv_b, dt_bias, a_log, d_skip, ssm_norm, expand)


def _merge_kernel(x_ref, mod_ref, npost_ref, ao_ref, ys_ref, ga_ref, gs_ref, wa_ref, ws_ref, wo_ref,
                  o_ref):
    w = npost_ref[...] * mod_ref[0, 5:6, :]
    for r0 in range(0, TM_MERGE, TR_MERGE):
        rows = slice(r0, r0 + TR_MERGE)
        y_a = _dot(ao_ref[rows, :], wa_ref[...])
        y_s = _dot(ys_ref[rows, :], ws_ref[...])
        merged = (_sigmoid(ga_ref[rows, :].astype(F32)) * y_a
                  + _sigmoid(gs_ref[rows, :].astype(F32)) * y_s)
        t = _dot(merged.astype(BF16), wo_ref[...])
        o_ref[rows, :] = x_ref[rows, :] + _rms_normalize(t) * w


def _merge(x, mod_l, npost, ao, ys, proj, wa, ws, wo, layer, seq):
    t = x.shape[0]
    tiles_per_seq = seq // TM_MERGE
    resident = pl.Buffered(1)
    return pl.pallas_call(
        _merge_kernel,
        grid=(t // TM_MERGE,),
        in_specs=[
            pl.BlockSpec((TM_MERGE, D_MODEL), lambda i: (i, 0)),
            pl.BlockSpec((1, N_MOD, D_MODEL), lambda i: (i // tiles_per_seq, 0, 0)),
            pl.BlockSpec((1, D_MODEL), lambda i: (0, 0)),
            pl.BlockSpec((TM_MERGE, ATTN_WIDTH), lambda i: (i, 0)),
            pl.BlockSpec((TM_MERGE, D_INNER), lambda i: (i, 0)),
            pl.BlockSpec((TM_MERGE, D_MODEL), lambda i: (i, COL_GA // D_MODEL)),
            pl.BlockSpec((TM_MERGE, D_MODEL), lambda i: (i, COL_GS // D_MODEL)),
            pl.BlockSpec((None, ATTN_WIDTH, D_MODEL), lambda i: (layer, 0, 0), pipeline_mode=resident),
            pl.BlockSpec((None, D_INNER, D_MODEL), lambda i: (layer, 0, 0), pipeline_mode=resident),
            pl.BlockSpec((None, D_MODEL, D_MODEL), lambda i: (layer, 0, 0), pipeline_mode=resident),
        ],
        out_specs=pl.BlockSpec((TM_MERGE, D_MODEL), lambda i: (i, 0)),
        out_shape=jax.ShapeDtypeStruct((t, D_MODEL), F32),
        compiler_params=pltpu.CompilerParams(
            dimension_semantics=("arbitrary",), vmem_limit_bytes=VMEM_LIMIT),
        name="merge_out",
    )(x, mod_l, npost, ao, ys, proj, proj, wa, ws, wo)


def _reorder_w_in(w_in):
    o = 0
    q = w_in[..., o:o + ATTN_WIDTH]; o += ATTN_WIDTH
    k = w_in[..., o:o + KV_WIDTH]; o += KV_WIDTH
    v = w_in[..., o:o + KV_WIDTH]; o += KV_WIDTH
    z = w_in[..., o:o + D_INNER]; o += D_INNER
    xs = w_in[..., o:o + D_INNER]; o += D_INNER
    bm = w_in[..., o:o + BC_WIDTH]; o += BC_WIDTH
    cm = w_in[..., o:o + BC_WIDTH]; o += BC_WIDTH
    dt = w_in[..., o:o + SSM_HEADS]; o += SSM_HEADS
    ga = w_in[..., o:o + D_MODEL]; o += D_MODEL
    gs = w_in[..., o:o + D_MODEL]

    def dup(w):
        w4 = w.reshape(DEPTH, D_MODEL, ATTN_KV_HEADS, HEAD_DIM)
        return jnp.concatenate([w4, w4], axis=-1).reshape(DEPTH, D_MODEL, 2 * KV_WIDTH)

    main = jnp.concatenate([z, ga, gs, xs, q, bm, cm, dup(k), dup(v)], axis=-1).astype(BF16)
    dt_w = jnp.pad(dt, ((0, 0), (0, 0), (0, LANES - SSM_HEADS))).astype(BF16)
    return main, dt_w


def _pad_heads(p):
    return jnp.pad(p, ((0, 0), (0, LANES - SSM_HEADS))).reshape(DEPTH, 1, LANES)


def kernel(x, c, w_mod, b_mod, norm_pre, norm_post, w_ffn_gate, w_ffn_up, w_ffn_down, w_in,
           attn_sinks, conv_w, conv_b, dt_bias, a_log, d_skip, ssm_norm, w_attn_out, w_ssm_out, w_out):
    batch, seq, _ = x.shape
    assert seq % max(TM_FFN, TM_PROJ, TQ_ATTN, TM_MERGE, CHUNK) == 0
    assert batch <= HALO_ROWS
    t = batch * seq

    c_pad = jnp.pad(c, ((0, HALO_ROWS - batch), (0, 0)))
    mod = _mod_all(c_pad, w_mod, b_mod).reshape(DEPTH, HALO_ROWS, N_MOD, D_MODEL)[:, :batch]

    wg = _cast_bf16(w_ffn_gate)
    wu = _cast_bf16(w_ffn_up)
    wd = _cast_bf16(w_ffn_down)
    w_main, w_dt = _reorder_w_in(w_in)
    wa = w_attn_out.astype(BF16)
    ws = w_ssm_out.astype(BF16)
    wo = w_out.astype(BF16)
    dtb = _pad_heads(dt_bias)
    alog = _pad_heads(a_log)
    dsk = _pad_heads(d_skip)
    lane_head = lax.broadcasted_iota(jnp.int32, (LANES, D_INNER), 1) // SSM_HEAD_DIM
    expand = (lane_head == lax.broadcasted_iota(jnp.int32, (LANES, D_INNER), 0)).astype(BF16)

    xf = x.reshape(t, D_MODEL)
    for l in range(DEPTH):
        npre = norm_pre[l].reshape(3, 1, D_MODEL)
        npost = norm_post[l].reshape(3, 1, D_MODEL)
        xf = _ffn(xf, mod[l], npre[0], npost[0], wg, wu, wd, l, 0, seq)
        proj, dt_raw = _inproj(xf, mod[l], npre[1], w_main, w_dt, l, seq)
        ao = _attention(proj, attn_sinks[l], batch, seq)
        ys = _ssd(proj, dt_raw, conv_w[l], conv_b[l].reshape(1, CONV_CH), dtb[l], alog[l], dsk[l],
                  ssm_norm[l].reshape(1, D_INNER), expand, batch, seq)
        xf = _merge(xf, mod[l], npost[1], ao, ys, proj, wa, ws, wo, l, seq)
        xf = _ffn(xf, mod[l], npre[2], npost[2], wg, wu, wd, l, 1, seq)
    return xf.reshape(batch, seq, D_MODEL)
```

```python
import functools

import jax
import jax.numpy as jnp
from jax import lax
from jax.experimental import pallas as pl
from jax.experimental.pallas import tpu as pltpu

F32 = jnp.float32
BF16 = jnp.bfloat16

D_MODEL = 2048
DEPTH = 4
ATTN_HEADS = 16
ATTN_KV_HEADS = 4
HEAD_DIM = 64
WINDOW = 128
ATTN_WIDTH = ATTN_HEADS * HEAD_DIM
KV_WIDTH = ATTN_KV_HEADS * HEAD_DIM
D_INNER = D_MODEL
SSM_HEAD_DIM = 64
SSM_HEADS = D_INNER // SSM_HEAD_DIM
SSM_GROUPS = 4
D_STATE = 128
CONV_WIDTH = 4
CHUNK = 128
BC_WIDTH = SSM_GROUPS * D_STATE
CONV_CH = D_INNER + 2 * BC_WIDTH
D_FF = 5632
N_MOD = 9
NORM_EPS = 1e-6
GROUP_WIDTH = D_INNER // SSM_GROUPS
HEADS_PER_GROUP = SSM_HEADS // SSM_GROUPS
Q_PER_KV = ATTN_HEADS // ATTN_KV_HEADS

LANES = 128
SUBLANES = 8
VMEM_LIMIT = 56 * 1024 * 1024

COL_Z = 0
COL_GA = COL_Z + D_INNER
COL_GS = COL_GA + D_MODEL
COL_XS = COL_GS + D_MODEL
COL_Q = COL_XS + D_INNER
COL_B = COL_Q + ATTN_WIDTH
COL_C = COL_B + BC_WIDTH
COL_KD = COL_C + BC_WIDTH
COL_VD = COL_KD + 2 * KV_WIDTH
P_MAIN = COL_VD + 2 * KV_WIDTH

TM_FFN = 1024
TR_FFN = 256
TF_FFN = 512
TM_PROJ = 1024
TN_PROJ = 1024
TQ_ATTN = 512
TM_MERGE = 512
TR_MERGE = 256
TM_CAST = 256
TN_MOD = 1024


def _dot(a, b):
    return jnp.dot(a, b, preferred_element_type=F32)


def _dot_nt(a, b):
    return lax.dot_general(a, b, (((1,), (1,)), ((), ())), preferred_element_type=F32)


def _sigmoid(x):
    return 1.0 / (1.0 + jnp.exp(-x))


def _silu(x):
    return x * _sigmoid(x)


def _rms_normalize(x):
    return x * lax.rsqrt(jnp.mean(x * x, axis=-1, keepdims=True) + NORM_EPS)


def _split3(v):
    hi = v.astype(BF16)
    r1 = v - hi.astype(F32)
    mid = r1.astype(BF16)
    lo = (r1 - mid.astype(F32)).astype(BF16)
    return hi, mid, lo


def _dot_exact_rhs(v, m):
    hi, mid, lo = _split3(v)
    return _dot(hi, m) + _dot(mid, m) + _dot(lo, m)


def _dot_exact_lhs(m, v):
    hi, mid, lo = _split3(v)
    return _dot(m, hi) + _dot(m, mid) + _dot(m, lo)


def _cast_kernel(w_ref, o_ref):
    o_ref[...] = w_ref[...].astype(BF16)


def _cast_bf16(w):
    cols = w.shape[-1]
    w2 = w.reshape(-1, cols)
    out = pl.pallas_call(
        _cast_kernel,
        grid=(w2.shape[0] // TM_CAST,),
        in_specs=[pl.BlockSpec((TM_CAST, cols), lambda i: (i, 0))],
        out_specs=pl.BlockSpec((TM_CAST, cols), lambda i: (i, 0)),
        out_shape=jax.ShapeDtypeStruct(w2.shape, BF16),
        compiler_params=pltpu.CompilerParams(
            dimension_semantics=("arbitrary",), vmem_limit_bytes=VMEM_LIMIT),
        name="cast_bf16",
    )(w2)
    return out.reshape(w.shape)


def _mod_kernel(c_ref, w_ref, b_ref, o_ref):
    c = c_ref[...]
    o_ref[0] = _dot(_silu(c).astype(BF16), w_ref[0].astype(BF16)) + b_ref[0]


def _mod_all(c_pad, w_mod, b_mod):
    rows = c_pad.shape[0]
    n = N_MOD * D_MODEL
    return pl.pallas_call(
        _mod_kernel,
        grid=(DEPTH, n // TN_MOD),
        in_specs=[
            pl.BlockSpec((rows, D_MODEL), lambda l, j: (0, 0)),
            pl.BlockSpec((1, D_MODEL, TN_MOD), lambda l, j: (l, 0, j)),
            pl.BlockSpec((1, 1, TN_MOD), lambda l, j: (l, 0, j)),
        ],
        out_specs=pl.BlockSpec((1, rows, TN_MOD), lambda l, j: (l, 0, j)),
        out_shape=jax.ShapeDtypeStruct((DEPTH, rows, n), F32),
        compiler_params=pltpu.CompilerParams(
            dimension_semantics=("arbitrary", "arbitrary"), vmem_limit_bytes=VMEM_LIMIT),
        name="mod_proj",
    )(c_pad, w_mod, b_mod.reshape(DEPTH, 1, n))


def _ffn_kernel(x_ref, mod_ref, npre_ref, npost_ref, wg_ref, wu_ref, wd_ref, o_ref, h_ref, *, mod_base):
    j = pl.program_id(1)
    last = pl.num_programs(1) - 1

    def step(first, final):
        if first:
            w_pre = npre_ref[...] * (1.0 + mod_ref[0, mod_base + 1:mod_base + 2, :])
            shift = mod_ref[0, mod_base:mod_base + 1, :]
        if final:
            w_post = npost_ref[...] * (0.5 * mod_ref[0, mod_base + 2:mod_base + 3, :])
        for r0 in range(0, TM_FFN, TR_FFN):
            rows = slice(r0, r0 + TR_FFN)
            if first:
                h = (_rms_normalize(x_ref[rows, :]) * w_pre + shift).astype(BF16)
                h_ref[rows, :] = h
            else:
                h = h_ref[rows, :]
            gate = _dot(h, wg_ref[...])
            up = _dot(h, wu_ref[...])
            acc = _dot((_silu(gate) * up).astype(BF16), wd_ref[...])
            if not first:
                acc = o_ref[rows, :] + acc
            if final:
                acc = x_ref[rows, :] + _rms_normalize(acc) * w_post
            o_ref[rows, :] = acc

    pl.when(j == 0)(functools.partial(step, True, False))
    pl.when((j > 0) & (j < last))(functools.partial(step, False, False))
    pl.when(j == last)(functools.partial(step, False, True))


def _ffn(x, mod_l, npre, npost, wg, wu, wd, layer, which, seq):
    t = x.shape[0]
    tiles_per_seq = seq // TM_FFN
    mod_base = 0 if which == 0 else 6
    kern = functools.partial(_ffn_kernel, mod_base=mod_base)
    return pl.pallas_call(
        kern,
        grid=(t // TM_FFN, D_FF // TF_FFN),
        in_specs=[
            pl.BlockSpec((TM_FFN, D_MODEL), lambda i, j: (i, 0)),
            pl.BlockSpec((1, N_MOD, D_MODEL), lambda i, j: (i // tiles_per_seq, 0, 0)),
            pl.BlockSpec((1, D_MODEL), lambda i, j: (0, 0)),
            pl.BlockSpec((1, D_MODEL), lambda i, j: (0, 0)),
            pl.BlockSpec((None, None, D_MODEL, TF_FFN), lambda i, j: (layer, which, 0, j)),
            pl.BlockSpec((None, None, D_MODEL, TF_FFN), lambda i, j: (layer, which, 0, j)),
            pl.BlockSpec((None, None, TF_FFN, D_MODEL), lambda i, j: (layer, which, j, 0)),
        ],
        out_specs=pl.BlockSpec((TM_FFN, D_MODEL), lambda i, j: (i, 0)),
        out_shape=jax.ShapeDtypeStruct((t, D_MODEL), F32),
        scratch_shapes=[pltpu.VMEM((TM_FFN, D_MODEL), BF16)],
        compiler_params=pltpu.CompilerParams(
            dimension_semantics=("arbitrary", "arbitrary"), vmem_limit_bytes=VMEM_LIMIT),
        name="ffn",
    )(x, mod_l, npre, npost, wg, wu, wd)


def _inproj_kernel(x_ref, mod_ref, npre_ref, w_ref, wdt_ref, o_ref, dt_ref, h_ref):
    j = pl.program_id(1)

    @pl.when(j == 0)
    def _():
        w = npre_ref[...] * (1.0 + mod_ref[0, 4:5, :])
        shift = mod_ref[0, 3:4, :]
        for r0 in range(0, TM_PROJ, TR_FFN):
            rows = slice(r0, r0 + TR_FFN)
            h = (_rms_normalize(x_ref[rows, :]) * w + shift).astype(BF16)
            h_ref[rows, :] = h
            dt_ref[rows, :] = _dot(h, wdt_ref[...])
            o_ref[rows, :] = _dot(h, w_ref[...]).astype(BF16)

    @pl.when(j > 0)
    def _():
        o_ref[...] = _dot(h_ref[...], w_ref[...]).astype(BF16)


def _inproj(x, mod_l, npre, w_main, w_dt, layer, seq):
    t = x.shape[0]
    tiles_per_seq = seq // TM_PROJ
    return pl.pallas_call(
        _inproj_kernel,
        grid=(t // TM_PROJ, P_MAIN // TN_PROJ),
        in_specs=[
            pl.BlockSpec((TM_PROJ, D_MODEL), lambda i, j: (i, 0)),
            pl.BlockSpec((1, N_MOD, D_MODEL), lambda i, j: (i // tiles_per_seq, 0, 0)),
            pl.BlockSpec((1, D_MODEL), lambda i, j: (0, 0)),
            pl.BlockSpec((None, D_MODEL, TN_PROJ), lambda i, j: (layer, 0, j)),
            pl.BlockSpec((None, D_MODEL, LANES), lambda i, j: (layer, 0, 0)),
        ],
        out_specs=[
            pl.BlockSpec((TM_PROJ, TN_PROJ), lambda i, j: (i, j)),
            pl.BlockSpec((TM_PROJ, LANES), lambda i, j: (i, 0)),
        ],
        out_shape=[jax.ShapeDtypeStruct((t, P_MAIN), BF16), jax.ShapeDtypeStruct((t, LANES), F32)],
        scratch_shapes=[pltpu.VMEM((TM_PROJ, D_MODEL), BF16)],
        compiler_params=pltpu.CompilerParams(
            dimension_semantics=("arbitrary", "arbitrary"), vmem_limit_bytes=VMEM_LIMIT),
        name="in_proj",
    )(x, mod_l, npre, w_main, w_dt)


def _attn_kernel(sink_ref, q_ref, kc_ref, kp_ref, vc_ref, vp_ref, o_ref, kk_ref, vlo_ref, vhi_ref,
                 bias_ref):
    i = pl.program_id(1)
    lo1 = lax.broadcasted_iota(jnp.int32, (1, LANES), 1) < HEAD_DIM
    zero = jnp.zeros((), BF16)
    kk_ref[0:WINDOW, :] = kp_ref[...]
    kk_ref[WINDOW:, :] = kc_ref[...]
    for g in range(ATTN_KV_HEADS):
        vp = vp_ref[:, g * LANES:(g + 1) * LANES]
        vc = vc_ref[:, g * LANES:(g + 1) * LANES]
        vcols = slice(2 * g * LANES, (2 * g + 1) * LANES)
        vlo_ref[0:WINDOW, vcols] = jnp.where(lo1, vp, zero)
        vlo_ref[WINDOW:, vcols] = jnp.where(lo1, vc, zero)
        vhi_ref[0:WINDOW, vcols] = jnp.where(lo1, zero, vp)
        vhi_ref[WINDOW:, vcols] = jnp.where(lo1, zero, vc)

    @pl.when((pl.program_id(0) == 0) & (i == 0))
    def _():
        lo_all = lax.broadcasted_iota(jnp.int32, (TQ_ATTN + WINDOW, LANES), 1) < HEAD_DIM
        ones_lo = jnp.where(lo_all, 1.0, 0.0).astype(BF16)
        ones_hi = jnp.where(lo_all, 0.0, 1.0).astype(BF16)
        for g in range(ATTN_KV_HEADS):
            ocols = slice((2 * g + 1) * LANES, (2 * g + 2) * LANES)
            vlo_ref[:, ocols] = ones_lo
            vhi_ref[:, ocols] = ones_hi
        row = lax.broadcasted_iota(jnp.int32, (WINDOW, 2 * WINDOW), 0)
        col = lax.broadcasted_iota(jnp.int32, (WINDOW, 2 * WINDOW), 1)
        dist = WINDOW + row - col
        band = (dist >= 0) & (dist < WINDOW)
        distf = dist.astype(F32)
        for h in range(ATTN_HEADS):
            slope = 2.0 ** (-8.0 * (h + 1) / ATTN_HEADS)
            bias_ref[0, h] = jnp.where(band, -slope * distf, -1e30)
            bias_ref[1, h] = jnp.where(band & (col >= WINDOW), -slope * distf, -1e30)

    q_scale = jnp.asarray(HEAD_DIM ** -0.5, BF16)
    lo_rows = lax.broadcasted_iota(jnp.int32, (WINDOW, LANES), 1) < HEAD_DIM

    def block(n, carry):
        r0 = pl.multiple_of(n * WINDOW, WINDOW)
        variant = jnp.where((i == 0) & (n == 0), 1, 0)
        for g in range(ATTN_KV_HEADS):
            kk = kk_ref[pl.ds(r0, 2 * WINDOW), g * LANES:(g + 1) * LANES]
            vlo = vlo_ref[pl.ds(r0, 2 * WINDOW), 2 * g * LANES:(2 * g + 2) * LANES]
            vhi = vhi_ref[pl.ds(r0, 2 * WINDOW), 2 * g * LANES:(2 * g + 2) * LANES]
            outs = []
            for pair in range(Q_PER_KV // 2):
                c0 = (g * Q_PER_KV + 2 * pair) * HEAD_DIM
                q2 = q_ref[pl.ds(r0, WINDOW), c0:c0 + LANES] * q_scale
                qs = jnp.concatenate([jnp.where(lo1, q2, zero), jnp.where(lo1, zero, q2)], axis=0)
                s = _dot_nt(qs, kk)
                weights = []
                sink_terms = []
                for hh in range(2):
                    h = g * Q_PER_KV + 2 * pair + hh
                    sh = s[hh * WINDOW:(hh + 1) * WINDOW] + bias_ref[variant, h]
                    sink = sink_ref[h]
                    m = jnp.maximum(jnp.max(sh, axis=-1, keepdims=True), sink)
                    weights.append(jnp.exp(sh - m).astype(BF16))
                    sink_terms.append(jnp.exp(sink - m))
                r = _dot(weights[0], vlo) + _dot(weights[1], vhi)
                den = r[:, LANES:] + jnp.where(lo_rows, sink_terms[0], sink_terms[1])
                outs.append(r[:, :LANES] * (1.0 / den))
            o_ref[pl.ds(r0, WINDOW), g * 2 * LANES:(g + 1) * 2 * LANES] = (
                jnp.concatenate(outs, axis=1).astype(BF16))
        return carry

    lax.fori_loop(0, TQ_ATTN // WINDOW, block, 0)


def _attention(proj, sinks, batch, seq):
    t = proj.shape[0]
    tiles = seq // TQ_ATTN
    blocks_per_tile = TQ_ATTN // WINDOW
    kd_blk = COL_KD // (2 * KV_WIDTH)
    vd_blk = COL_VD // (2 * KV_WIDTH)

    def cur(col):
        return lambda b, i: (b * tiles + i, col)

    def prev(col):
        return lambda b, i: (jnp.maximum((b * tiles + i) * blocks_per_tile - 1, 0), col)

    return pl.pallas_call(
        _attn_kernel,
        grid=(batch, tiles),
        in_specs=[
            pl.BlockSpec(memory_space=pltpu.SMEM),
            pl.BlockSpec((TQ_ATTN, ATTN_WIDTH), cur(COL_Q // ATTN_WIDTH)),
            pl.BlockSpec((TQ_ATTN, 2 * KV_WIDTH), cur(kd_blk)),
            pl.BlockSpec((WINDOW, 2 * KV_WIDTH), prev(kd_blk)),
            pl.BlockSpec((TQ_ATTN, 2 * KV_WIDTH), cur(vd_blk)),
            pl.BlockSpec((WINDOW, 2 * KV_WIDTH), prev(vd_blk)),
        ],
        out_specs=pl.BlockSpec((TQ_ATTN, ATTN_WIDTH), lambda b, i: (b * tiles + i, 0)),
        out_shape=jax.ShapeDtypeStruct((t, ATTN_WIDTH), BF16),
        scratch_shapes=[
            pltpu.VMEM((TQ_ATTN + WINDOW, 2 * KV_WIDTH), BF16),
            pltpu.VMEM((TQ_ATTN + WINDOW, 4 * KV_WIDTH), BF16),
            pltpu.VMEM((TQ_ATTN + WINDOW, 4 * KV_WIDTH), BF16),
            pltpu.VMEM((2, ATTN_HEADS, WINDOW, 2 * WINDOW), F32),
        ],
        compiler_params=pltpu.CompilerParams(
            dimension_semantics=("arbitrary", "arbitrary"), vmem_limit_bytes=VMEM_LIMIT),
        name="swa_attention",
    )(sinks, proj, proj, proj, proj, proj)


def _ssd_kernel(z_ref, xs_ref, b_ref, c_ref, dt_ref, cw_ref, cb_ref, dtb_ref, alog_ref, dskip_ref,
                snorm_ref, e_ref, shift_ref, o_ref, raw_ref, state_ref, xc_ref):
    chunk = pl.program_id(1)

    @pl.when(chunk == 0)
    def _():
        raw_ref[0:CHUNK, :] = jnp.zeros((CHUNK, CONV_CH), BF16)
        state_ref[...] = jnp.zeros_like(state_ref)

    raw_ref[CHUNK:, 0:D_INNER] = xs_ref[...]
    raw_ref[CHUNK:, D_INNER:D_INNER + BC_WIDTH] = b_ref[...]
    raw_ref[CHUNK:, D_INNER + BC_WIDTH:] = c_ref[...]

    for c0 in range(0, CONV_CH, GROUP_WIDTH):
        cols = slice(c0, c0 + GROUP_WIDTH)
        delayed = _dot(shift_ref[...], raw_ref[:, cols])
        acc = raw_ref[CHUNK:, cols].astype(F32) * cw_ref[CONV_WIDTH - 1:CONV_WIDTH, cols] + cb_ref[:, cols]
        for s in range(1, CONV_WIDTH):
            acc = acc + delayed[(s - 1) * CHUNK:s * CHUNK] * cw_ref[CONV_WIDTH - 1 - s:CONV_WIDTH - s, cols]
        xc_ref[:, cols] = _silu(acc)
    raw_ref[0:CHUNK, :] = raw_ref[CHUNK:, :]

    head_lane = lax.broadcasted_iota(jnp.int32, (1, LANES), 1) < SSM_HEADS
    pre = dt_ref[...] + dtb_ref[...]
    dt = jnp.maximum(pre, 0.0) + jnp.log(1.0 + jnp.exp(-jnp.abs(pre)))
    dta = jnp.where(head_lane, dt * (-jnp.exp(alog_ref[...])), 0.0)
    ri = lax.broadcasted_iota(jnp.int32, (CHUNK, CHUNK), 0)
    ci = lax.broadcasted_iota(jnp.int32, (CHUNK, CHUNK), 1)
    causal = ri >= ci
    acs = _dot_exact_lhs(causal.astype(BF16), dta)
    acs_t = acs.T
    acs_last = acs[CHUNK - 1:CHUNK, :]
    row_factors = jnp.concatenate([dt, dt * jnp.exp(acs_last - acs), jnp.exp(acs)], axis=0).astype(BF16)
    head_factors = jnp.concatenate([
        jnp.broadcast_to(jnp.exp(acs_last), (SUBLANES, LANES)),
        jnp.broadcast_to(dskip_ref[...], (SUBLANES, LANES)),
    ], axis=0)
    lo1 = lax.broadcasted_iota(jnp.int32, (1, LANES), 1) < SSM_HEAD_DIM

    for g in range(SSM_GROUPS):
        gcols = slice(g * GROUP_WIDTH, (g + 1) * GROUP_WIDTH)
        xg = xc_ref[:, gcols]
        bg = xc_ref[:, D_INNER + g * D_STATE:D_INNER + (g + 1) * D_STATE]
        cg = xc_ref[:, D_INNER + BC_WIDTH + g * D_STATE:D_INNER + BC_WIDTH + (g + 1) * D_STATE]
        bb = bg.astype(BF16)
        cbf = cg.astype(BF16)
        ex = _dot(row_factors, e_ref[:, gcols])
        dt_e = ex[0:CHUNK]
        dtdec_e = ex[CHUNK:2 * CHUNK]
        eacs_e = ex[2 * CHUNK:3 * CHUNK]
        exh = _dot_exact_rhs(head_factors, e_ref[:, gcols])
        cdec_e = exh[0:1]
        dskip_e = exh[SUBLANES:SUBLANES + 1]
        xdt = xg * dt_e

        cb = _dot_nt(cbf, bb)
        st = state_ref[:, gcols]
        y_off = _dot(cbf, st.astype(BF16)) * eacs_e
        new_state = _dot(bg.T.astype(BF16), (xg * dtdec_e).astype(BF16))
        state_ref[:, gcols] = st * cdec_e + new_state

        pairs = []
        for p in range(HEADS_PER_GROUP // 2):
            pcols = slice(p * LANES, (p + 1) * LANES)
            ms = []
            for hh in range(2):
                h = g * HEADS_PER_GROUP + 2 * p + hh
                diff = acs[:, h:h + 1] - acs_t[h:h + 1, :]
                decay = jnp.exp(jnp.where(causal, diff, -1e30))
                ms.append((cb * decay).astype(BF16))
            xp = xdt[:, pcols]
            rhs = jnp.concatenate([jnp.where(lo1, xp, 0.0), jnp.where(lo1, 0.0, xp)], axis=0)
            y_diag = _dot(jnp.concatenate(ms, axis=1), rhs.astype(BF16))
            pairs.append(y_diag + y_off[:, pcols] + dskip_e[:, pcols] * xg[:, pcols])
        y = jnp.concatenate(pairs, axis=1)
        y = y * _silu(z_ref[:, gcols].astype(F32))
        o_ref[:, gcols] = (_rms_normalize(y) * snorm_ref[:, gcols]).astype(BF16)


def _ssd(proj, dt_raw, conv_w, conv_b, dt_bias, a_log, d_skip, ssm_norm, expand, shift, batch, seq):
    t = proj.shape[0]
    chunks = seq // CHUNK

    def rows(col):
        return lambda b, c: (b * chunks + c, col)

    def const(b, c):
        return (0, 0)

    return pl.pallas_call(
        _ssd_kernel,
        grid=(batch, chunks),
        in_specs=[
            pl.BlockSpec((CHUNK, D_INNER), rows(COL_Z // D_INNER)),
            pl.BlockSpec((CHUNK, D_INNER), rows(COL_XS // D_INNER)),
            pl.BlockSpec((CHUNK, BC_WIDTH), rows(COL_B // BC_WIDTH)),
            pl.BlockSpec((CHUNK, BC_WIDTH), rows(COL_C // BC_WIDTH)),
            pl.BlockSpec((CHUNK, LANES), rows(0)),
            pl.BlockSpec((CONV_WIDTH, CONV_CH), const),
            pl.BlockSpec((1, CONV_CH), const),
            pl.BlockSpec((1, LANES), const),
            pl.BlockSpec((1, LANES), const),
            pl.BlockSpec((1, LANES), const),
            pl.BlockSpec((1, D_INNER), const),
            pl.BlockSpec((LANES, D_INNER), const),
            pl.BlockSpec(((CONV_WIDTH - 1) * CHUNK, 2 * CHUNK), const),
        ],
        out_specs=pl.BlockSpec((CHUNK, D_INNER), lambda b, c: (b * chunks + c, 0)),
        out_shape=jax.ShapeDtypeStruct((t, D_INNER), BF16),
        scratch_shapes=[
            pltpu.VMEM((2 * CHUNK, CONV_CH), BF16),
            pltpu.VMEM((D_STATE, D_INNER), F32),
            pltpu.VMEM((CHUNK, CONV_CH), F32),
        ],
        compiler_params=pltpu.CompilerParams(
            dimension_semantics=("arbitrary", "arbitrary"), vmem_limit_bytes=VMEM_LIMIT),
        name="ssd",
    )(proj, proj, proj, proj, dt_raw, conv_w, conv_b, dt_bias, a_log, d_skip, ssm_norm, expand, shift)


def _merge_kernel(x_ref, mod_ref, npost_ref, ao_ref, ys_ref, ga_ref, gs_ref, wa_ref, ws_ref, wo_ref,
                  o_ref):
    w = npost_ref[...] * mod_ref[0, 5:6, :]
    for r0 in range(0, TM_MERGE, TR_MERGE):
        rows = slice(r0, r0 + TR_MERGE)
        y_a = _dot(ao_ref[rows, :], wa_ref[...])
        y_s = _dot(ys_ref[rows, :], ws_ref[...])
        merged = (_sigmoid(ga_ref[rows, :].astype(F32)) * y_a
                  + _sigmoid(gs_ref[rows, :].astype(F32)) * y_s)
        t = _dot(merged.astype(BF16), wo_ref[...])
        o_ref[rows, :] = x_ref[rows, :] + _rms_normalize(t) * w


def _merge(x, mod_l, npost, ao, ys, proj, wa, ws, wo, layer, seq):
    t = x.shape[0]
    tiles_per_seq = seq // TM_MERGE
    resident = pl.Buffered(1)
    return pl.pallas_call(
        _merge_kernel,
        grid=(t // TM_MERGE,),
        in_specs=[
            pl.BlockSpec((TM_MERGE, D_MODEL), lambda i: (i, 0)),
            pl.BlockSpec((1, N_MOD, D_MODEL), lambda i: (i // tiles_per_seq, 0, 0)),
            pl.BlockSpec((1, D_MODEL), lambda i: (0, 0)),
            pl.BlockSpec((TM_MERGE, ATTN_WIDTH), lambda i: (i, 0)),
            pl.BlockSpec((TM_MERGE, D_INNER), lambda i: (i, 0)),
            pl.BlockSpec((TM_MERGE, D_MODEL), lambda i: (i, COL_GA // D_MODEL)),
            pl.BlockSpec((TM_MERGE, D_MODEL), lambda i: (i, COL_GS // D_MODEL)),
            pl.BlockSpec((None, ATTN_WIDTH, D_MODEL), lambda i: (layer, 0, 0), pipeline_mode=resident),
            pl.BlockSpec((None, D_INNER, D_MODEL), lambda i: (layer, 0, 0), pipeline_mode=resident),
            pl.BlockSpec((None, D_MODEL, D_MODEL), lambda i: (layer, 0, 0), pipeline_mode=resident),
        ],
        out_specs=pl.BlockSpec((TM_MERGE, D_MODEL), lambda i: (i, 0)),
        out_shape=jax.ShapeDtypeStruct((t, D_MODEL), F32),
        compiler_params=pltpu.CompilerParams(
            dimension_semantics=("arbitrary",), vmem_limit_bytes=VMEM_LIMIT),
        name="merge_out",
    )(x, mod_l, npost, ao, ys, proj, proj, wa, ws, wo)


def _reorder_w_in(w_in):
    o = 0
    q = w_in[..., o:o + ATTN_WIDTH]; o += ATTN_WIDTH
    k = w_in[..., o:o + KV_WIDTH]; o += KV_WIDTH
    v = w_in[..., o:o + KV_WIDTH]; o += KV_WIDTH
    z = w_in[..., o:o + D_INNER]; o += D_INNER
    xs = w_in[..., o:o + D_INNER]; o += D_INNER
    bm = w_in[..., o:o + BC_WIDTH]; o += BC_WIDTH
    cm = w_in[..., o:o + BC_WIDTH]; o += BC_WIDTH
    dt = w_in[..., o:o + SSM_HEADS]; o += SSM_HEADS
    ga = w_in[..., o:o + D_MODEL]; o += D_MODEL
    gs = w_in[..., o:o + D_MODEL]

    def dup(w):
        w4 = w.reshape(DEPTH, D_MODEL, ATTN_KV_HEADS, HEAD_DIM)
        return jnp.concatenate([w4, w4], axis=-1).reshape(DEPTH, D_MODEL, 2 * KV_WIDTH)

    main = jnp.concatenate([z, ga, gs, xs, q, bm, cm, dup(k), dup(v)], axis=-1).astype(BF16)
    dt_w = jnp.pad(dt, ((0, 0), (0, 0), (0, LANES - SSM_HEADS))).astype(BF16)
    return main, dt_w


def _pad_heads(p):
    return jnp.pad(p, ((0, 0), (0, LANES - SSM_HEADS))).reshape(DEPTH, 1, LANES)


def kernel(x, c, w_mod, b_mod, norm_pre, norm_post, w_ffn_gate, w_ffn_up, w_ffn_down, w_in,
           attn_sinks, conv_w, conv_b, dt_bias, a_log, d_skip, ssm_norm, w_attn_out, w_ssm_out, w_out):
    batch, seq, _ = x.shape
    assert seq % max(TM_FFN, TM_PROJ, TQ_ATTN, TM_MERGE, CHUNK) == 0
    assert batch <= SUBLANES
    t = batch * seq

    c_pad = jnp.pad(c, ((0, SUBLANES - batch), (0, 0)))
    mod = _mod_all(c_pad, w_mod, b_mod).reshape(DEPTH, SUBLANES, N_MOD, D_MODEL)[:, :batch]

    wg = _cast_bf16(w_ffn_gate)
    wu = _cast_bf16(w_ffn_up)
    wd = _cast_bf16(w_ffn_down)
    w_main, w_dt = _reorder_w_in(w_in)
    wa = w_attn_out.astype(BF16)
    ws = w_ssm_out.astype(BF16)
    wo = w_out.astype(BF16)
    dtb = _pad_heads(dt_bias)
    alog = _pad_heads(a_log)
    dsk = _pad_heads(d_skip)
    lane_head = lax.broadcasted_iota(jnp.int32, (LANES, D_INNER), 1) // SSM_HEAD_DIM
    expand = (lane_head == lax.broadcasted_iota(jnp.int32, (LANES, D_INNER), 0)).astype(BF16)
    out_row = lax.broadcasted_iota(jnp.int32, (CONV_WIDTH - 1, CHUNK, 2 * CHUNK), 1)
    delay = lax.broadcasted_iota(jnp.int32, (CONV_WIDTH - 1, CHUNK, 2 * CHUNK), 0) + 1
    src_row = lax.broadcasted_iota(jnp.int32, (CONV_WIDTH - 1, CHUNK, 2 * CHUNK), 2)
    shift = (src_row == CHUNK + out_row - delay).astype(BF16).reshape((CONV_WIDTH - 1) * CHUNK, 2 * CHUNK)

    xf = x.reshape(t, D_MODEL)
    for l in range(DEPTH):
        npre = norm_pre[l].reshape(3, 1, D_MODEL)
        npost = norm_post[l].reshape(3, 1, D_MODEL)
        xf = _ffn(xf, mod[l], npre[0], npost[0], wg, wu, wd, l, 0, seq)
        proj, dt_raw = _inproj(xf, mod[l], npre[1], w_main, w_dt, l, seq)
        ao = _attention(proj, attn_sinks[l], batch, seq)
        ys = _ssd(proj, dt_raw, conv_w[l], conv_b[l].reshape(1, CONV_CH), dtb[l], alog[l], dsk[l],
                  ssm_norm[l].reshape(1, D_INNER), expand, shift, batch, seq)
        xf = _merge(xf, mod[l], npost[1], ao, ys, proj, wa, ws, wo, l, seq)
        xf = _ffn(xf, mod[l], npre[2], npost[2], wg, wu, wd, l, 1, seq)
    return xf.reshape(batch, seq, D_MODEL)
```

```python
import functools

import jax
import jax.numpy as jnp
from jax import lax
from jax.experimental import pallas as pl
from jax.experimental.pallas import tpu as pltpu

F32 = jnp.float32
BF16 = jnp.bfloat16

D_MODEL = 2048
DEPTH = 4
ATTN_HEADS = 16
ATTN_KV_HEADS = 4
HEAD_DIM = 64
WINDOW = 128
ATTN_WIDTH = ATTN_HEADS * HEAD_DIM
KV_WIDTH = ATTN_KV_HEADS * HEAD_DIM
D_INNER = D_MODEL
SSM_HEAD_DIM = 64
SSM_HEADS = D_INNER // SSM_HEAD_DIM
SSM_GROUPS = 4
D_STATE = 128
CONV_WIDTH = 4
CHUNK = 128
BC_WIDTH = SSM_GROUPS * D_STATE
CONV_CH = D_INNER + 2 * BC_WIDTH
D_FF = 5632
N_MOD = 9
NORM_EPS = 1e-6
GROUP_WIDTH = D_INNER // SSM_GROUPS
HEADS_PER_GROUP = SSM_HEADS // SSM_GROUPS
Q_PER_KV = ATTN_HEADS // ATTN_KV_HEADS

LANES = 128
SUBLANES = 8
VMEM_LIMIT = 56 * 1024 * 1024
VMEM_LIMIT_MIX = 58 * 1024 * 1024

COL_Z = 0
COL_GA = COL_Z + D_INNER
COL_GS = COL_GA + D_MODEL
COL_XS = COL_GS + D_MODEL
COL_Q = COL_XS + D_INNER
COL_B = COL_Q + ATTN_WIDTH
COL_C = COL_B + BC_WIDTH
COL_KD = COL_C + BC_WIDTH
COL_VD = COL_KD + 2 * KV_WIDTH
P_MAIN = COL_VD + 2 * KV_WIDTH

TM_FFN = 1024
TR_FFN = 256
TF_FFN = 512
TM_PROJ = 1024
TN_PROJ = 1024
TM_MIX = 256
TM_CAST = 256
TN_MOD = 1024


def _dot(a, b):
    return jnp.dot(a, b, preferred_element_type=F32)


def _dot_nt(a, b):
    return lax.dot_general(a, b, (((1,), (1,)), ((), ())), preferred_element_type=F32)


def _sigmoid(x):
    return 1.0 / (1.0 + jnp.exp(-x))


def _silu(x):
    return x * _sigmoid(x)


def _rms_normalize(x):
    return x * lax.rsqrt(jnp.mean(x * x, axis=-1, keepdims=True) + NORM_EPS)


def _split3(v):
    hi = v.astype(BF16)
    r1 = v - hi.astype(F32)
    mid = r1.astype(BF16)
    lo = (r1 - mid.astype(F32)).astype(BF16)
    return hi, mid, lo


def _dot_exact_rhs(v, m):
    hi, mid, lo = _split3(v)
    return _dot(hi, m) + _dot(mid, m) + _dot(lo, m)


def _dot_exact_lhs(m, v):
    hi, mid, lo = _split3(v)
    return _dot(m, hi) + _dot(m, mid) + _dot(m, lo)


def _cast_kernel(w_ref, o_ref):
    o_ref[...] = w_ref[...].astype(BF16)


def _cast_bf16(w):
    cols = w.shape[-1]
    w2 = w.reshape(-1, cols)
    out = pl.pallas_call(
        _cast_kernel,
        grid=(w2.shape[0] // TM_CAST,),
        in_specs=[pl.BlockSpec((TM_CAST, cols), lambda i: (i, 0))],
        out_specs=pl.BlockSpec((TM_CAST, cols), lambda i: (i, 0)),
        out_shape=jax.ShapeDtypeStruct(w2.shape, BF16),
        compiler_params=pltpu.CompilerParams(
            dimension_semantics=("arbitrary",), vmem_limit_bytes=VMEM_LIMIT),
        name="cast_bf16",
    )(w2)
    return out.reshape(w.shape)


def _mod_kernel(c_ref, w_ref, b_ref, o_ref):
    c = c_ref[...]
    o_ref[0] = _dot(_silu(c).astype(BF16), w_ref[0].astype(BF16)) + b_ref[0]


def _mod_all(c_pad, w_mod, b_mod):
    rows = c_pad.shape[0]
    n = N_MOD * D_MODEL
    return pl.pallas_call(
        _mod_kernel,
        grid=(DEPTH, n // TN_MOD),
        in_specs=[
            pl.BlockSpec((rows, D_MODEL), lambda l, j: (0, 0)),
            pl.BlockSpec((1, D_MODEL, TN_MOD), lambda l, j: (l, 0, j)),
            pl.BlockSpec((1, 1, TN_MOD), lambda l, j: (l, 0, j)),
        ],
        out_specs=pl.BlockSpec((1, rows, TN_MOD), lambda l, j: (l, 0, j)),
        out_shape=jax.ShapeDtypeStruct((DEPTH, rows, n), F32),
        compiler_params=pltpu.CompilerParams(
            dimension_semantics=("arbitrary", "arbitrary"), vmem_limit_bytes=VMEM_LIMIT),
        name="mod_proj",
    )(c_pad, w_mod, b_mod.reshape(DEPTH, 1, n))


def _ffn_kernel(x_ref, mod_ref, npre_ref, npost_ref, wg_ref, wu_ref, wd_ref, o_ref, h_ref, *, mod_base):
    j = pl.program_id(1)
    last = pl.num_programs(1) - 1

    def step(first, final):
        if first:
            w_pre = npre_ref[...] * (1.0 + mod_ref[0, mod_base + 1:mod_base + 2, :])
            shift = mod_ref[0, mod_base:mod_base + 1, :]
        if final:
            w_post = npost_ref[...] * (0.5 * mod_ref[0, mod_base + 2:mod_base + 3, :])
        for r0 in range(0, TM_FFN, TR_FFN):
            rows = slice(r0, r0 + TR_FFN)
            if first:
                h = (_rms_normalize(x_ref[rows, :]) * w_pre + shift).astype(BF16)
                h_ref[rows, :] = h
            else:
                h = h_ref[rows, :]
            gate = _dot(h, wg_ref[...])
            up = _dot(h, wu_ref[...])
            acc = _dot((_silu(gate) * up).astype(BF16), wd_ref[...])
            if not first:
                acc = o_ref[rows, :] + acc
            if final:
                acc = x_ref[rows, :] + _rms_normalize(acc) * w_post
            o_ref[rows, :] = acc

    pl.when(j == 0)(functools.partial(step, True, False))
    pl.when((j > 0) & (j < last))(functools.partial(step, False, False))
    pl.when(j == last)(functools.partial(step, False, True))


def _ffn(x, mod_l, npre, npost, wg, wu, wd, layer, which, seq):
    t = x.shape[0]
    tiles_per_seq = seq // TM_FFN
    mod_base = 0 if which == 0 else 6
    kern = functools.partial(_ffn_kernel, mod_base=mod_base)
    return pl.pallas_call(
        kern,
        grid=(t // TM_FFN, D_FF // TF_FFN),
        in_specs=[
            pl.BlockSpec((TM_FFN, D_MODEL), lambda i, j: (i, 0)),
            pl.BlockSpec((1, N_MOD, D_MODEL), lambda i, j: (i // tiles_per_seq, 0, 0)),
            pl.BlockSpec((1, D_MODEL), lambda i, j: (0, 0)),
            pl.BlockSpec((1, D_MODEL), lambda i, j: (0, 0)),
            pl.BlockSpec((None, None, D_MODEL, TF_FFN), lambda i, j: (layer, which, 0, j)),
            pl.BlockSpec((None, None, D_MODEL, TF_FFN), lambda i, j: (layer, which, 0, j)),
            pl.BlockSpec((None, None, TF_FFN, D_MODEL), lambda i, j: (layer, which, j, 0)),
        ],
        out_specs=pl.BlockSpec((TM_FFN, D_MODEL), lambda i, j: (i, 0)),
        out_shape=jax.ShapeDtypeStruct((t, D_MODEL), F32),
        scratch_shapes=[pltpu.VMEM((TM_FFN, D_MODEL), BF16)],
        compiler_params=pltpu.CompilerParams(
            dimension_semantics=("arbitrary", "arbitrary"), vmem_limit_bytes=VMEM_LIMIT),
        name="ffn",
    )(x, mod_l, npre, npost, wg, wu, wd)


def _inproj_kernel(x_ref, mod_ref, npre_ref, w_ref, wdt_ref, o_ref, dt_ref, h_ref):
    j = pl.program_id(1)

    @pl.when(j == 0)
    def _():
        w = npre_ref[...] * (1.0 + mod_ref[0, 4:5, :])
        shift = mod_ref[0, 3:4, :]
        for r0 in range(0, TM_PROJ, TR_FFN):
            rows = slice(r0, r0 + TR_FFN)
            h = (_rms_normalize(x_ref[rows, :]) * w + shift).astype(BF16)
            h_ref[rows, :] = h
            dt_ref[rows, :] = _dot(h, wdt_ref[...])
            o_ref[rows, :] = _dot(h, w_ref[...]).astype(BF16)

    @pl.when(j > 0)
    def _():
        o_ref[...] = _dot(h_ref[...], w_ref[...]).astype(BF16)


def _inproj(x, mod_l, npre, w_main, w_dt, layer, seq):
    t = x.shape[0]
    tiles_per_seq = seq // TM_PROJ
    return pl.pallas_call(
        _inproj_kernel,
        grid=(t // TM_PROJ, P_MAIN // TN_PROJ),
        in_specs=[
            pl.BlockSpec((TM_PROJ, D_MODEL), lambda i, j: (i, 0)),
            pl.BlockSpec((1, N_MOD, D_MODEL), lambda i, j: (i // tiles_per_seq, 0, 0)),
            pl.BlockSpec((1, D_MODEL), lambda i, j: (0, 0)),
            pl.BlockSpec((None, D_MODEL, TN_PROJ), lambda i, j: (layer, 0, j)),
            pl.BlockSpec((None, D_MODEL, LANES), lambda i, j: (layer, 0, 0)),
        ],
        out_specs=[
            pl.BlockSpec((TM_PROJ, TN_PROJ), lambda i, j: (i, j)),
            pl.BlockSpec((TM_PROJ, LANES), lambda i, j: (i, 0)),
        ],
        out_shape=[jax.ShapeDtypeStruct((t, P_MAIN), BF16), jax.ShapeDtypeStruct((t, LANES), F32)],
        scratch_shapes=[pltpu.VMEM((TM_PROJ, D_MODEL), BF16)],
        compiler_params=pltpu.CompilerParams(
            dimension_semantics=("arbitrary", "arbitrary"), vmem_limit_bytes=VMEM_LIMIT),
        name="in_proj",
    )(x, mod_l, npre, w_main, w_dt)


def _attn_chunk(first, rows, sink_ref, q_ref, kc_ref, vc_ref, o_ref, kk_ref, vlo_ref, vhi_ref, bias_ref):
    lo1 = lax.broadcasted_iota(jnp.int32, (1, LANES), 1) < HEAD_DIM
    zero = jnp.zeros((), BF16)
    vcols = [slice(2 * g * LANES, (2 * g + 1) * LANES) for g in range(ATTN_KV_HEADS)]

    kk_ref[WINDOW:, :] = kc_ref[rows, :]
    for g in range(ATTN_KV_HEADS):
        vc = vc_ref[rows, g * LANES:(g + 1) * LANES]
        vlo_ref[WINDOW:, vcols[g]] = jnp.where(lo1, vc, zero)
        vhi_ref[WINDOW:, vcols[g]] = jnp.where(lo1, zero, vc)

    q_scale = jnp.asarray(HEAD_DIM ** -0.5, BF16)
    lo_rows = lax.broadcasted_iota(jnp.int32, (WINDOW, LANES), 1) < HEAD_DIM
    variant = jnp.where(first, 1, 0)
    for g in range(ATTN_KV_HEADS):
        kk = kk_ref[:, g * LANES:(g + 1) * LANES]
        vlo = vlo_ref[:, 2 * g * LANES:(2 * g + 2) * LANES]
        vhi = vhi_ref[:, 2 * g * LANES:(2 * g + 2) * LANES]
        outs = []
        for pair in range(Q_PER_KV // 2):
            c0 = (g * Q_PER_KV + 2 * pair) * HEAD_DIM
            q2 = q_ref[rows, c0:c0 + LANES] * q_scale
            qs = jnp.concatenate([jnp.where(lo1, q2, zero), jnp.where(lo1, zero, q2)], axis=0)
            s = _dot_nt(qs, kk)
            weights = []
            sink_terms = []
            for hh in range(2):
                h = g * Q_PER_KV + 2 * pair + hh
                sh = s[hh * WINDOW:(hh + 1) * WINDOW] + bias_ref[variant, h]
                sink = sink_ref[h]
                m = jnp.maximum(jnp.max(sh, axis=-1, keepdims=True), sink)
                weights.append(jnp.exp(sh - m).astype(BF16))
                sink_terms.append(jnp.exp(sink - m))
            r = _dot(weights[0], vlo) + _dot(weights[1], vhi)
            den = r[:, LANES:] + jnp.where(lo_rows, sink_terms[0], sink_terms[1])
            outs.append(r[:, :LANES] * (1.0 / den))
        o_ref[rows, g * 2 * LANES:(g + 1) * 2 * LANES] = jnp.concatenate(outs, axis=1).astype(BF16)

    kk_ref[0:WINDOW, :] = kk_ref[WINDOW:, :]
    for g in range(ATTN_KV_HEADS):
        vlo_ref[0:WINDOW, vcols[g]] = vlo_ref[WINDOW:, vcols[g]]
        vhi_ref[0:WINDOW, vcols[g]] = vhi_ref[WINDOW:, vcols[g]]


def _attn_constants(vlo_ref, vhi_ref, bias_ref):
    lo_all = lax.broadcasted_iota(jnp.int32, (2 * WINDOW, LANES), 1) < HEAD_DIM
    ones_lo = jnp.where(lo_all, 1.0, 0.0).astype(BF16)
    ones_hi = jnp.where(lo_all, 0.0, 1.0).astype(BF16)
    for g in range(ATTN_KV_HEADS):
        ocols = slice((2 * g + 1) * LANES, (2 * g + 2) * LANES)
        vlo_ref[:, ocols] = ones_lo
        vhi_ref[:, ocols] = ones_hi
    row = lax.broadcasted_iota(jnp.int32, (WINDOW, 2 * WINDOW), 0)
    col = lax.broadcasted_iota(jnp.int32, (WINDOW, 2 * WINDOW), 1)
    dist = WINDOW + row - col
    band = (dist >= 0) & (dist < WINDOW)
    distf = dist.astype(F32)
    for h in range(ATTN_HEADS):
        slope = 2.0 ** (-8.0 * (h + 1) / ATTN_HEADS)
        bias_ref[0, h] = jnp.where(band, -slope * distf, -1e30)
        bias_ref[1, h] = jnp.where(band & (col >= WINDOW), -slope * distf, -1e30)


def _ssd_chunk(next_first, rows, z_ref, xs_ref, b_ref, c_ref, dt_ref, cw_ref, cb_ref, dtb_ref, alog_ref,
               dskip_ref, snorm_ref, e_ref, shift_ref, o_ref, raw_ref, state_ref, xc_ref):
    keep = jnp.where(next_first, 0.0, 1.0)
    raw_ref[CHUNK:, 0:D_INNER] = xs_ref[rows, :]
    raw_ref[CHUNK:, D_INNER:D_INNER + BC_WIDTH] = b_ref[rows, :]
    raw_ref[CHUNK:, D_INNER + BC_WIDTH:] = c_ref[rows, :]

    for c0 in range(0, CONV_CH, GROUP_WIDTH):
        cols = slice(c0, c0 + GROUP_WIDTH)
        delayed = _dot(shift_ref[...], raw_ref[:, cols])
        acc = raw_ref[CHUNK:, cols].astype(F32) * cw_ref[CONV_WIDTH - 1:CONV_WIDTH, cols] + cb_ref[:, cols]
        for s in range(1, CONV_WIDTH):
            acc = acc + delayed[(s - 1) * CHUNK:s * CHUNK] * cw_ref[CONV_WIDTH - 1 - s:CONV_WIDTH - s, cols]
        xc_ref[:, cols] = _silu(acc)
    raw_ref[0:CHUNK, :] = raw_ref[CHUNK:, :] * keep.astype(BF16)

    head_lane = lax.broadcasted_iota(jnp.int32, (1, LANES), 1) < SSM_HEADS
    pre = dt_ref[rows, :] + dtb_ref[...]
    dt = jnp.maximum(pre, 0.0) + jnp.log(1.0 + jnp.exp(-jnp.abs(pre)))
    dta = jnp.where(head_lane, dt * (-jnp.exp(alog_ref[...])), 0.0)
    ri = lax.broadcasted_iota(jnp.int32, (CHUNK, CHUNK), 0)
    ci = lax.broadcasted_iota(jnp.int32, (CHUNK, CHUNK), 1)
    causal = ri >= ci
    acs = _dot_exact_lhs(causal.astype(BF16), dta)
    acs_t = acs.T
    acs_last = acs[CHUNK - 1:CHUNK, :]
    row_factors = jnp.concatenate([dt, dt * jnp.exp(acs_last - acs), jnp.exp(acs)], axis=0).astype(BF16)
    head_factors = jnp.concatenate([
        jnp.broadcast_to(jnp.exp(acs_last), (SUBLANES, LANES)),
        jnp.broadcast_to(dskip_ref[...], (SUBLANES, LANES)),
    ], axis=0)
    lo1 = lax.broadcasted_iota(jnp.int32, (CHUNK, LANES), 1) < SSM_HEAD_DIM

    for g in range(SSM_GROUPS):
        gcols = slice(g * GROUP_WIDTH, (g + 1) * GROUP_WIDTH)
        xg = xc_ref[:, gcols]
        bg = xc_ref[:, D_INNER + g * D_STATE:D_INNER + (g + 1) * D_STATE]
        cg = xc_ref[:, D_INNER + BC_WIDTH + g * D_STATE:D_INNER + BC_WIDTH + (g + 1) * D_STATE]
        bb = bg.astype(BF16)
        cbf = cg.astype(BF16)
        ex = _dot(row_factors, e_ref[:, gcols])
        dt_e = ex[0:CHUNK]
        dtdec_e = ex[CHUNK:2 * CHUNK]
        eacs_e = ex[2 * CHUNK:3 * CHUNK]
        exh = _dot_exact_rhs(head_factors, e_ref[:, gcols])
        cdec_e = exh[0:1]
        dskip_e = exh[SUBLANES:SUBLANES + 1]
        xdt = xg * dt_e

        cb = _dot_nt(cbf, bb)
        st = state_ref[:, gcols]
        y_off = _dot(cbf, st.astype(BF16)) * eacs_e
        new_state = _dot(bg.T.astype(BF16), (xg * dtdec_e).astype(BF16))
        state_ref[:, gcols] = (st * cdec_e + new_state) * keep

        pairs = []
        for p in range(HEADS_PER_GROUP // 2):
            pcols = slice(p * LANES, (p + 1) * LANES)
            ms = []
            for hh in range(2):
                h = g * HEADS_PER_GROUP + 2 * p + hh
                diff = acs[:, h:h + 1] - acs_t[h:h + 1, :]
                decay = jnp.exp(jnp.where(causal, diff, -1e30))
                ms.append((cb * decay).astype(BF16))
            xp = xdt[:, pcols]
            rhs = jnp.concatenate([jnp.where(lo1, xp, 0.0), jnp.where(lo1, 0.0, xp)], axis=0)
            y_diag = _dot(jnp.concatenate(ms, axis=1), rhs.astype(BF16))
            pairs.append(y_diag + y_off[:, pcols] + dskip_e[:, pcols] * xg[:, pcols])
        y = jnp.concatenate(pairs, axis=1)
        y = y * _silu(z_ref[rows, gcols].astype(F32))
        o_ref[rows, gcols] = (_rms_normalize(y) * snorm_ref[:, gcols]).astype(BF16)


def _mixmerge_kernel(sink_ref, q_ref, kc_ref, vc_ref, z_ref, xs_ref, b_ref, c_ref, dt_ref, cw_ref, cb_ref,
                     dtb_ref, alog_ref, dskip_ref, snorm_ref, e_ref, shift_ref,
                     x_ref, mod_ref, npost_ref, ga_ref, gs_ref, wa_ref, ws_ref, wo_ref, o_ref,
                     ao_ref, ys_ref, ao_prev_ref, ys_prev_ref, kk_ref, vlo_ref, vhi_ref, bias_ref,
                     raw_ref, state_ref, xc_ref, *, chunks_per_seq):
    s = pl.program_id(0)
    last = pl.num_programs(0) - 1

    def mixers():
        for sub in range(TM_MIX // CHUNK):
            rows = slice(sub * CHUNK, (sub + 1) * CHUNK)
            chunk = s * (TM_MIX // CHUNK) + sub
            first = lax.rem(chunk, chunks_per_seq) == 0
            next_first = lax.rem(chunk + 1, chunks_per_seq) == 0
            _attn_chunk(first, rows, sink_ref, q_ref, kc_ref, vc_ref, ao_ref, kk_ref, vlo_ref, vhi_ref,
                        bias_ref)
            _ssd_chunk(next_first, rows, z_ref, xs_ref, b_ref, c_ref, dt_ref, cw_ref, cb_ref, dtb_ref,
                       alog_ref, dskip_ref, snorm_ref, e_ref, shift_ref, ys_ref, raw_ref, state_ref, xc_ref)

    def merge():
        y_a = _dot(ao_prev_ref[...], wa_ref[...])
        y_s = _dot(ys_prev_ref[...], ws_ref[...])
        merged = _sigmoid(ga_ref[...].astype(F32)) * y_a + _sigmoid(gs_ref[...].astype(F32)) * y_s
        t = _dot(merged.astype(BF16), wo_ref[...])
        o_ref[...] = x_ref[...] + _rms_normalize(t) * (npost_ref[...] * mod_ref[0, 5:6, :])

    def hand_over():
        ao_prev_ref[...] = ao_ref[...]
        ys_prev_ref[...] = ys_ref[...]

    @pl.when(s == 0)
    def _():
        _attn_constants(vlo_ref, vhi_ref, bias_ref)
        kk_ref[0:WINDOW, :] = jnp.zeros((WINDOW, 2 * KV_WIDTH), BF16)
        for g in range(ATTN_KV_HEADS):
            vlo_ref[0:WINDOW, 2 * g * LANES:(2 * g + 1) * LANES] = jnp.zeros((WINDOW, LANES), BF16)
            vhi_ref[0:WINDOW, 2 * g * LANES:(2 * g + 1) * LANES] = jnp.zeros((WINDOW, LANES), BF16)
        raw_ref[0:CHUNK, :] = jnp.zeros((CHUNK, CONV_CH), BF16)
        state_ref[...] = jnp.zeros_like(state_ref)
        mixers()

    @pl.when((s > 0) & (s < last))
    def _():
        hand_over()
        merge()
        mixers()

    @pl.when(s == last)
    def _():
        hand_over()
        merge()


def _mixmerge(x, mod_l, npost, proj, dt_raw, sinks, conv_w, conv_b, dt_bias, a_log, d_skip, ssm_norm,
              expand, shift, wa, ws, wo, layer, seq):
    t = x.shape[0]
    tiles = t // TM_MIX

    def side(col):
        return lambda s: (jnp.minimum(s, tiles - 1), col)

    def lag(col):
        return lambda s: (jnp.maximum(s - 1, 0), col)

    def const(s):
        return (0, 0)

    resident = pl.Buffered(1)
    kern = functools.partial(_mixmerge_kernel, chunks_per_seq=seq // CHUNK)
    return pl.pallas_call(
        kern,
        grid=(tiles + 1,),
        in_specs=[
            pl.BlockSpec(memory_space=pltpu.SMEM),
            pl.BlockSpec((TM_MIX, ATTN_WIDTH), side(COL_Q // ATTN_WIDTH)),
            pl.BlockSpec((TM_MIX, 2 * KV_WIDTH), side(COL_KD // (2 * KV_WIDTH))),
            pl.BlockSpec((TM_MIX, 2 * KV_WIDTH), side(COL_VD // (2 * KV_WIDTH))),
            pl.BlockSpec((TM_MIX, D_INNER), side(COL_Z // D_INNER)),
            pl.BlockSpec((TM_MIX, D_INNER), side(COL_XS // D_INNER)),
            pl.BlockSpec((TM_MIX, BC_WIDTH), side(COL_B // BC_WIDTH)),
            pl.BlockSpec((TM_MIX, BC_WIDTH), side(COL_C // BC_WIDTH)),
            pl.BlockSpec((TM_MIX, LANES), side(0)),
            pl.BlockSpec((CONV_WIDTH, CONV_CH), const),
            pl.BlockSpec((1, CONV_CH), const),
            pl.BlockSpec((1, LANES), const),
            pl.BlockSpec((1, LANES), const),
            pl.BlockSpec((1, LANES), const),
            pl.BlockSpec((1, D_INNER), const),
            pl.BlockSpec((LANES, D_INNER), const),
            pl.BlockSpec(((CONV_WIDTH - 1) * CHUNK, 2 * CHUNK), const),
            pl.BlockSpec((TM_MIX, D_MODEL), lag(0)),
            pl.BlockSpec((1, N_MOD, D_MODEL), lambda s: (jnp.maximum(s - 1, 0) * TM_MIX // seq, 0, 0)),
            pl.BlockSpec((1, D_MODEL), const),
            pl.BlockSpec((TM_MIX, D_MODEL), lag(COL_GA // D_MODEL)),
            pl.BlockSpec((TM_MIX, D_MODEL), lag(COL_GS // D_MODEL)),
            pl.BlockSpec((None, ATTN_WIDTH, D_MODEL), lambda s: (layer, 0, 0), pipeline_mode=resident),
            pl.BlockSpec((None, D_INNER, D_MODEL), lambda s: (layer, 0, 0), pipeline_mode=resident),
            pl.BlockSpec((None, D_MODEL, D_MODEL), lambda s: (layer, 0, 0), pipeline_mode=resident),
        ],
        out_specs=pl.BlockSpec((TM_MIX, D_MODEL), lag(0)),
        out_shape=jax.ShapeDtypeStruct((t, D_MODEL), F32),
        scratch_shapes=[
            pltpu.VMEM((TM_MIX, ATTN_WIDTH), BF16),
            pltpu.VMEM((TM_MIX, D_INNER), BF16),
            pltpu.VMEM((TM_MIX, ATTN_WIDTH), BF16),
            pltpu.VMEM((TM_MIX, D_INNER), BF16),
            pltpu.VMEM((2 * WINDOW, 2 * KV_WIDTH), BF16),
            pltpu.VMEM((2 * WINDOW, 4 * KV_WIDTH), BF16),
            pltpu.VMEM((2 * WINDOW, 4 * KV_WIDTH), BF16),
            pltpu.VMEM((2, ATTN_HEADS, WINDOW, 2 * WINDOW), F32),
            pltpu.VMEM((2 * CHUNK, CONV_CH), BF16),
            pltpu.VMEM((D_STATE, D_INNER), F32),
            pltpu.VMEM((CHUNK, CONV_CH), F32),
        ],
        compiler_params=pltpu.CompilerParams(
            dimension_semantics=("arbitrary",), vmem_limit_bytes=VMEM_LIMIT_MIX),
        name="mixers_merge",
    )(sinks, proj, proj, proj, proj, proj, proj, proj, dt_raw, conv_w, conv_b, dt_bias, a_log, d_skip,
      ssm_norm, expand, shift, x, mod_l, npost, proj, proj, wa, ws, wo)


def _reorder_w_in(w_in):
    o = 0
    q = w_in[..., o:o + ATTN_WIDTH]; o += ATTN_WIDTH
    k = w_in[..., o:o + KV_WIDTH]; o += KV_WIDTH
    v = w_in[..., o:o + KV_WIDTH]; o += KV_WIDTH
    z = w_in[..., o:o + D_INNER]; o += D_INNER
    xs = w_in[..., o:o + D_INNER]; o += D_INNER
    bm = w_in[..., o:o + BC_WIDTH]; o += BC_WIDTH
    cm = w_in[..., o:o + BC_WIDTH]; o += BC_WIDTH
    dt = w_in[..., o:o + SSM_HEADS]; o += SSM_HEADS
    ga = w_in[..., o:o + D_MODEL]; o += D_MODEL
    gs = w_in[..., o:o + D_MODEL]

    def dup(w):
        w4 = w.reshape(DEPTH, D_MODEL, ATTN_KV_HEADS, HEAD_DIM)
        return jnp.concatenate([w4, w4], axis=-1).reshape(DEPTH, D_MODEL, 2 * KV_WIDTH)

    main = jnp.concatenate([z, ga, gs, xs, q, bm, cm, dup(k), dup(v)], axis=-1).astype(BF16)
    dt_w = jnp.pad(dt, ((0, 0), (0, 0), (0, LANES - SSM_HEADS))).astype(BF16)
    return main, dt_w


def _pad_heads(p):
    return jnp.pad(p, ((0, 0), (0, LANES - SSM_HEADS))).reshape(DEPTH, 1, LANES)


def kernel(x, c, w_mod, b_mod, norm_pre, norm_post, w_ffn_gate, w_ffn_up, w_ffn_down, w_in,
           attn_sinks, conv_w, conv_b, dt_bias, a_log, d_skip, ssm_norm, w_attn_out, w_ssm_out, w_out):
    batch, seq, _ = x.shape
    assert seq % max(TM_FFN, TM_PROJ, TM_MIX, CHUNK) == 0 and WINDOW == CHUNK
    assert batch <= SUBLANES
    t = batch * seq

    c_pad = jnp.pad(c, ((0, SUBLANES - batch), (0, 0)))
    mod = _mod_all(c_pad, w_mod, b_mod).reshape(DEPTH, SUBLANES, N_MOD, D_MODEL)[:, :batch]

    wg = _cast_bf16(w_ffn_gate)
    wu = _cast_bf16(w_ffn_up)
    wd = _cast_bf16(w_ffn_down)
    w_main, w_dt = _reorder_w_in(w_in)
    wa = w_attn_out.astype(BF16)
    ws = w_ssm_out.astype(BF16)
    wo = w_out.astype(BF16)
    dtb = _pad_heads(dt_bias)
    alog = _pad_heads(a_log)
    dsk = _pad_heads(d_skip)
    lane_head = lax.broadcasted_iota(jnp.int32, (LANES, D_INNER), 1) // SSM_HEAD_DIM
    expand = (lane_head == lax.broadcasted_iota(jnp.int32, (LANES, D_INNER), 0)).astype(BF16)
    out_row = lax.broadcasted_iota(jnp.int32, (CONV_WIDTH - 1, CHUNK, 2 * CHUNK), 1)
    delay = lax.broadcasted_iota(jnp.int32, (CONV_WIDTH - 1, CHUNK, 2 * CHUNK), 0) + 1
    src_row = lax.broadcasted_iota(jnp.int32, (CONV_WIDTH - 1, CHUNK, 2 * CHUNK), 2)
    shift = (src_row == CHUNK + out_row - delay).astype(BF16).reshape((CONV_WIDTH - 1) * CHUNK, 2 * CHUNK)

    xf = x.reshape(t, D_MODEL)
    for l in range(DEPTH):
        npre = norm_pre[l].reshape(3, 1, D_MODEL)
        npost = norm_post[l].reshape(3, 1, D_MODEL)
        xf = _ffn(xf, mod[l], npre[0], npost[0], wg, wu, wd, l, 0, seq)
        proj, dt_raw = _inproj(xf, mod[l], npre[1], w_main, w_dt, l, seq)
        xf = _mixmerge(xf, mod[l], npost[1], proj, dt_raw, attn_sinks[l], conv_w[l],
                       conv_b[l].reshape(1, CONV_CH), dtb[l], alog[l], dsk[l],
                       ssm_norm[l].reshape(1, D_INNER), expand, shift, wa, ws, wo, l, seq)
        xf = _ffn(xf, mod[l], npre[2], npost[2], wg, wu, wd, l, 1, seq)
    return xf.reshape(batch, seq, D_MODEL)
```

```python
import functools

import jax
import jax.numpy as jnp
from jax import lax
from jax.experimental import pallas as pl
from jax.experimental.pallas import tpu as pltpu

F32 = jnp.float32
BF16 = jnp.bfloat16

D_MODEL = 2048
DEPTH = 4
ATTN_HEADS = 16
ATTN_KV_HEADS = 4
HEAD_DIM = 64
WINDOW = 128
ATTN_WIDTH = ATTN_HEADS * HEAD_DIM
KV_WIDTH = ATTN_KV_HEADS * HEAD_DIM
D_INNER = D_MODEL
SSM_HEAD_DIM = 64
SSM_HEADS = D_INNER // SSM_HEAD_DIM
SSM_GROUPS = 4
D_STATE = 128
CONV_WIDTH = 4
CHUNK = 128
BC_WIDTH = SSM_GROUPS * D_STATE
CONV_CH = D_INNER + 2 * BC_WIDTH
D_FF = 5632
N_MOD = 9
NORM_EPS = 1e-6
GROUP_WIDTH = D_INNER // SSM_GROUPS
HEADS_PER_GROUP = SSM_HEADS // SSM_GROUPS
Q_PER_KV = ATTN_HEADS // ATTN_KV_HEADS

LANES = 128
SUBLANES = 8
VMEM_LIMIT = 56 * 1024 * 1024
VMEM_LIMIT_MIX = 58 * 1024 * 1024

COL_Z = 0
COL_GA = COL_Z + D_INNER
COL_GS = COL_GA + D_MODEL
COL_XS = COL_GS + D_MODEL
COL_Q = COL_XS + D_INNER
COL_B = COL_Q + ATTN_WIDTH
COL_C = COL_B + BC_WIDTH
COL_KD = COL_C + BC_WIDTH
COL_VD = COL_KD + 2 * KV_WIDTH
P_MAIN = COL_VD + 2 * KV_WIDTH

TM_FFN = 1024
TR_FFN = 1024
TR_FFN_EDGE = 512
TR_PROJ = 256
TF_FFN = 512
TM_PROJ = 1024
TN_PROJ = 1024
TM_MIX = 256
TM_CAST = 256
TN_MOD = 1024


def _dot(a, b):
    return jnp.dot(a, b, preferred_element_type=F32)


def _dot_nt(a, b):
    return lax.dot_general(a, b, (((1,), (1,)), ((), ())), preferred_element_type=F32)


def _sigmoid(x):
    return 1.0 / (1.0 + jnp.exp(-x))


def _silu(x):
    return x * _sigmoid(x)


def _rms_normalize(x):
    return x * lax.rsqrt(jnp.mean(x * x, axis=-1, keepdims=True) + NORM_EPS)


def _split3(v):
    hi = v.astype(BF16)
    r1 = v - hi.astype(F32)
    mid = r1.astype(BF16)
    lo = (r1 - mid.astype(F32)).astype(BF16)
    return hi, mid, lo


def _dot_exact_rhs(v, m):
    hi, mid, lo = _split3(v)
    return _dot(hi, m) + _dot(mid, m) + _dot(lo, m)


def _dot_exact_lhs(m, v):
    hi, mid, lo = _split3(v)
    return _dot(m, hi) + _dot(m, mid) + _dot(m, lo)


def _cast_kernel(w_ref, o_ref):
    o_ref[...] = w_ref[...].astype(BF16)


def _cast_bf16(w):
    cols = w.shape[-1]
    w2 = w.reshape(-1, cols)
    out = pl.pallas_call(
        _cast_kernel,
        grid=(w2.shape[0] // TM_CAST,),
        in_specs=[pl.BlockSpec((TM_CAST, cols), lambda i: (i, 0))],
        out_specs=pl.BlockSpec((TM_CAST, cols), lambda i: (i, 0)),
        out_shape=jax.ShapeDtypeStruct(w2.shape, BF16),
        compiler_params=pltpu.CompilerParams(
            dimension_semantics=("arbitrary",), vmem_limit_bytes=VMEM_LIMIT),
        name="cast_bf16",
    )(w2)
    return out.reshape(w.shape)


def _mod_kernel(c_ref, w_ref, b_ref, o_ref):
    c = c_ref[...]
    o_ref[0] = _dot(_silu(c).astype(BF16), w_ref[0].astype(BF16)) + b_ref[0]


def _mod_all(c_pad, w_mod, b_mod):
    rows = c_pad.shape[0]
    n = N_MOD * D_MODEL
    return pl.pallas_call(
        _mod_kernel,
        grid=(DEPTH, n // TN_MOD),
        in_specs=[
            pl.BlockSpec((rows, D_MODEL), lambda l, j: (0, 0)),
            pl.BlockSpec((1, D_MODEL, TN_MOD), lambda l, j: (l, 0, j)),
            pl.BlockSpec((1, 1, TN_MOD), lambda l, j: (l, 0, j)),
        ],
        out_specs=pl.BlockSpec((1, rows, TN_MOD), lambda l, j: (l, 0, j)),
        out_shape=jax.ShapeDtypeStruct((DEPTH, rows, n), F32),
        compiler_params=pltpu.CompilerParams(
            dimension_semantics=("arbitrary", "arbitrary"), vmem_limit_bytes=VMEM_LIMIT),
        name="mod_proj",
    )(c_pad, w_mod, b_mod.reshape(DEPTH, 1, n))


def _ffn_kernel(x_ref, mod_ref, npre_ref, npost_ref, wg_ref, wu_ref, wd_ref, o_ref, h_ref, *, mod_base):
    j = pl.program_id(1)
    last = pl.num_programs(1) - 1

    def step(first, final):
        if first:
            w_pre = npre_ref[...] * (1.0 + mod_ref[0, mod_base + 1:mod_base + 2, :])
            shift = mod_ref[0, mod_base:mod_base + 1, :]
        if final:
            w_post = npost_ref[...] * (0.5 * mod_ref[0, mod_base + 2:mod_base + 3, :])
        chain = TR_FFN_EDGE if (first or final) else TR_FFN
        for r0 in range(0, TM_FFN, chain):
            rows = slice(r0, r0 + chain)
            if first:
                h = (_rms_normalize(x_ref[rows, :]) * w_pre + shift).astype(BF16)
                h_ref[rows, :] = h
            else:
                h = h_ref[rows, :]
            gate = _dot(h, wg_ref[...])
            up = _dot(h, wu_ref[...])
            acc = _dot((_silu(gate) * up).astype(BF16), wd_ref[...])
            if not first:
                acc = o_ref[rows, :] + acc
            if final:
                acc = x_ref[rows, :] + _rms_normalize(acc) * w_post
            o_ref[rows, :] = acc

    pl.when(j == 0)(functools.partial(step, True, False))
    pl.when((j > 0) & (j < last))(functools.partial(step, False, False))
    pl.when(j == last)(functools.partial(step, False, True))


def _ffn(x, mod_l, npre, npost, wg, wu, wd, layer, which, seq):
    t = x.shape[0]
    tiles_per_seq = seq // TM_FFN
    mod_base = 0 if which == 0 else 6
    kern = functools.partial(_ffn_kernel, mod_base=mod_base)
    return pl.pallas_call(
        kern,
        grid=(t // TM_FFN, D_FF // TF_FFN),
        in_specs=[
            pl.BlockSpec((TM_FFN, D_MODEL), lambda i, j: (i, 0)),
            pl.BlockSpec((1, N_MOD, D_MODEL), lambda i, j: (i // tiles_per_seq, 0, 0)),
            pl.BlockSpec((1, D_MODEL), lambda i, j: (0, 0)),
            pl.BlockSpec((1, D_MODEL), lambda i, j: (0, 0)),
            pl.BlockSpec((None, None, D_MODEL, TF_FFN), lambda i, j: (layer, which, 0, j)),
            pl.BlockSpec((None, None, D_MODEL, TF_FFN), lambda i, j: (layer, which, 0, j)),
            pl.BlockSpec((None, None, TF_FFN, D_MODEL), lambda i, j: (layer, which, j, 0)),
        ],
        out_specs=pl.BlockSpec((TM_FFN, D_MODEL), lambda i, j: (i, 0)),
        out_shape=jax.ShapeDtypeStruct((t, D_MODEL), F32),
        scratch_shapes=[pltpu.VMEM((TM_FFN, D_MODEL), BF16)],
        compiler_params=pltpu.CompilerParams(
            dimension_semantics=("arbitrary", "arbitrary"), vmem_limit_bytes=VMEM_LIMIT),
        name="ffn",
    )(x, mod_l, npre, npost, wg, wu, wd)


def _inproj_kernel(x_ref, mod_ref, npre_ref, w_ref, wdt_ref, o_ref, dt_ref, h_ref):
    j = pl.program_id(1)

    @pl.when(j == 0)
    def _():
        w = npre_ref[...] * (1.0 + mod_ref[0, 4:5, :])
        shift = mod_ref[0, 3:4, :]
        for r0 in range(0, TM_PROJ, TR_PROJ):
            rows = slice(r0, r0 + TR_PROJ)
            h = (_rms_normalize(x_ref[rows, :]) * w + shift).astype(BF16)
            h_ref[rows, :] = h
            dt_ref[rows, :] = _dot(h, wdt_ref[...])
            o_ref[rows, :] = _dot(h, w_ref[...]).astype(BF16)

    @pl.when(j > 0)
    def _():
        o_ref[...] = _dot(h_ref[...], w_ref[...]).astype(BF16)


def _inproj(x, mod_l, npre, w_main, w_dt, layer, seq):
    t = x.shape[0]
    tiles_per_seq = seq // TM_PROJ
    return pl.pallas_call(
        _inproj_kernel,
        grid=(t // TM_PROJ, P_MAIN // TN_PROJ),
        in_specs=[
            pl.BlockSpec((TM_PROJ, D_MODEL), lambda i, j: (i, 0)),
            pl.BlockSpec((1, N_MOD, D_MODEL), lambda i, j: (i // tiles_per_seq, 0, 0)),
            pl.BlockSpec((1, D_MODEL), lambda i, j: (0, 0)),
            pl.BlockSpec((None, D_MODEL, TN_PROJ), lambda i, j: (layer, 0, j)),
            pl.BlockSpec((None, D_MODEL, LANES), lambda i, j: (layer, 0, 0)),
        ],
        out_specs=[
            pl.BlockSpec((TM_PROJ, TN_PROJ), lambda i, j: (i, j)),
            pl.BlockSpec((TM_PROJ, LANES), lambda i, j: (i, 0)),
        ],
        out_shape=[jax.ShapeDtypeStruct((t, P_MAIN), BF16), jax.ShapeDtypeStruct((t, LANES), F32)],
        scratch_shapes=[pltpu.VMEM((TM_PROJ, D_MODEL), BF16)],
        compiler_params=pltpu.CompilerParams(
            dimension_semantics=("arbitrary", "arbitrary"), vmem_limit_bytes=VMEM_LIMIT),
        name="in_proj",
    )(x, mod_l, npre, w_main, w_dt)


def _attn_chunk(first, rows, sink_ref, q_ref, kc_ref, vc_ref, o_ref, kk_ref, vlo_ref, vhi_ref, bias_ref):
    lo1 = lax.broadcasted_iota(jnp.int32, (1, LANES), 1) < HEAD_DIM
    zero = jnp.zeros((), BF16)
    vcols = [slice(2 * g * LANES, (2 * g + 1) * LANES) for g in range(ATTN_KV_HEADS)]

    kk_ref[WINDOW:, :] = kc_ref[rows, :]
    for g in range(ATTN_KV_HEADS):
        vc = vc_ref[rows, g * LANES:(g + 1) * LANES]
        vlo_ref[WINDOW:, vcols[g]] = jnp.where(lo1, vc, zero)
        vhi_ref[WINDOW:, vcols[g]] = jnp.where(lo1, zero, vc)

    q_scale = jnp.asarray(HEAD_DIM ** -0.5, BF16)
    lo_rows = lax.broadcasted_iota(jnp.int32, (WINDOW, LANES), 1) < HEAD_DIM
    variant = jnp.where(first, 1, 0)
    for g in range(ATTN_KV_HEADS):
        kk = kk_ref[:, g * LANES:(g + 1) * LANES]
        vlo = vlo_ref[:, 2 * g * LANES:(2 * g + 2) * LANES]
        vhi = vhi_ref[:, 2 * g * LANES:(2 * g + 2) * LANES]
        outs = []
        for pair in range(Q_PER_KV // 2):
            c0 = (g * Q_PER_KV + 2 * pair) * HEAD_DIM
            q2 = q_ref[rows, c0:c0 + LANES] * q_scale
            qs = jnp.concatenate([jnp.where(lo1, q2, zero), jnp.where(lo1, zero, q2)], axis=0)
            s = _dot_nt(qs, kk)
            weights = []
            sink_terms = []
            for hh in range(2):
                h = g * Q_PER_KV + 2 * pair + hh
                sh = s[hh * WINDOW:(hh + 1) * WINDOW] + bias_ref[variant, h]
                sink = sink_ref[h]
                m = jnp.maximum(jnp.max(sh, axis=-1, keepdims=True), sink)
                weights.append(jnp.exp(sh - m).astype(BF16))
                sink_terms.append(jnp.exp(sink - m))
            r = _dot(weights[0], vlo) + _dot(weights[1], vhi)
            den = r[:, LANES:] + jnp.where(lo_rows, sink_terms[0], sink_terms[1])
            outs.append(r[:, :LANES] * (1.0 / den))
        o_ref[rows, g * 2 * LANES:(g + 1) * 2 * LANES] = jnp.concatenate(outs, axis=1).astype(BF16)

    kk_ref[0:WINDOW, :] = kk_ref[WINDOW:, :]
    for g in range(ATTN_KV_HEADS):
        vlo_ref[0:WINDOW, vcols[g]] = vlo_ref[WINDOW:, vcols[g]]
        vhi_ref[0:WINDOW, vcols[g]] = vhi_ref[WINDOW:, vcols[g]]


def _attn_constants(vlo_ref, vhi_ref, bias_ref):
    lo_all = lax.broadcasted_iota(jnp.int32, (2 * WINDOW, LANES), 1) < HEAD_DIM
    ones_lo = jnp.where(lo_all, 1.0, 0.0).astype(BF16)
    ones_hi = jnp.where(lo_all, 0.0, 1.0).astype(BF16)
    for g in range(ATTN_KV_HEADS):
        ocols = slice((2 * g + 1) * LANES, (2 * g + 2) * LANES)
        vlo_ref[:, ocols] = ones_lo
        vhi_ref[:, ocols] = ones_hi
    row = lax.broadcasted_iota(jnp.int32, (WINDOW, 2 * WINDOW), 0)
    col = lax.broadcasted_iota(jnp.int32, (WINDOW, 2 * WINDOW), 1)
    dist = WINDOW + row - col
    band = (dist >= 0) & (dist < WINDOW)
    distf = dist.astype(F32)
    for h in range(ATTN_HEADS):
        slope = 2.0 ** (-8.0 * (h + 1) / ATTN_HEADS)
        bias_ref[0, h] = jnp.where(band, -slope * distf, -1e30)
        bias_ref[1, h] = jnp.where(band & (col >= WINDOW), -slope * distf, -1e30)


def _ssd_chunk(next_first, rows, z_ref, xs_ref, b_ref, c_ref, dt_ref, cw_ref, cb_ref, dtb_ref, alog_ref,
               dskip_ref, snorm_ref, e_ref, shift_ref, o_ref, raw_ref, state_ref, xc_ref):
    keep = jnp.where(next_first, 0.0, 1.0)
    raw_ref[CHUNK:, 0:D_INNER] = xs_ref[rows, :]
    raw_ref[CHUNK:, D_INNER:D_INNER + BC_WIDTH] = b_ref[rows, :]
    raw_ref[CHUNK:, D_INNER + BC_WIDTH:] = c_ref[rows, :]

    for c0 in range(0, CONV_CH, GROUP_WIDTH):
        cols = slice(c0, c0 + GROUP_WIDTH)
        delayed = _dot(shift_ref[...], raw_ref[:, cols])
        acc = raw_ref[CHUNK:, cols].astype(F32) * cw_ref[CONV_WIDTH - 1:CONV_WIDTH, cols] + cb_ref[:, cols]
        for s in range(1, CONV_WIDTH):
            acc = acc + delayed[(s - 1) * CHUNK:s * CHUNK] * cw_ref[CONV_WIDTH - 1 - s:CONV_WIDTH - s, cols]
        xc_ref[:, cols] = _silu(acc)
    raw_ref[0:CHUNK, :] = raw_ref[CHUNK:, :] * keep.astype(BF16)

    head_lane = lax.broadcasted_iota(jnp.int32, (1, LANES), 1) < SSM_HEADS
    pre = dt_ref[rows, :] + dtb_ref[...]
    dt = jnp.maximum(pre, 0.0) + jnp.log(1.0 + jnp.exp(-jnp.abs(pre)))
    dta = jnp.where(head_lane, dt * (-jnp.exp(alog_ref[...])), 0.0)
    ri = lax.broadcasted_iota(jnp.int32, (CHUNK, CHUNK), 0)
    ci = lax.broadcasted_iota(jnp.int32, (CHUNK, CHUNK), 1)
    causal = ri >= ci
    acs = _dot_exact_lhs(causal.astype(BF16), dta)
    acs_t = acs.T
    acs_last = acs[CHUNK - 1:CHUNK, :]
    row_factors = jnp.concatenate([dt, dt * jnp.exp(acs_last - acs), jnp.exp(acs)], axis=0).astype(BF16)
    head_factors = jnp.concatenate([
        jnp.broadcast_to(jnp.exp(acs_last), (SUBLANES, LANES)),
        jnp.broadcast_to(dskip_ref[...], (SUBLANES, LANES)),
    ], axis=0)
    lo1 = lax.broadcasted_iota(jnp.int32, (CHUNK, LANES), 1) < SSM_HEAD_DIM

    for g in range(SSM_GROUPS):
        gcols = slice(g * GROUP_WIDTH, (g + 1) * GROUP_WIDTH)
        xg = xc_ref[:, gcols]
        bg = xc_ref[:, D_INNER + g * D_STATE:D_INNER + (g + 1) * D_STATE]
        cg = xc_ref[:, D_INNER + BC_WIDTH + g * D_STATE:D_INNER + BC_WIDTH + (g + 1) * D_STATE]
        bb = bg.astype(BF16)
        cbf = cg.astype(BF16)
        ex = _dot(row_factors, e_ref[:, gcols])
        dt_e = ex[0:CHUNK]
        dtdec_e = ex[CHUNK:2 * CHUNK]
        eacs_e = ex[2 * CHUNK:3 * CHUNK]
        exh = _dot_exact_rhs(head_factors, e_ref[:, gcols])
        cdec_e = exh[0:1]
        dskip_e = exh[SUBLANES:SUBLANES + 1]
        xdt = xg * dt_e

        cb = _dot_nt(cbf, bb)
        st = state_ref[:, gcols]
        y_off = _dot(cbf, st.astype(BF16)) * eacs_e
        new_state = _dot(bg.T.astype(BF16), (xg * dtdec_e).astype(BF16))
        state_ref[:, gcols] = (st * cdec_e + new_state) * keep

        pairs = []
        for p in range(HEADS_PER_GROUP // 2):
            pcols = slice(p * LANES, (p + 1) * LANES)
            ms = []
            for hh in range(2):
                h = g * HEADS_PER_GROUP + 2 * p + hh
                diff = acs[:, h:h + 1] - acs_t[h:h + 1, :]
                decay = jnp.exp(jnp.where(causal, diff, -1e30))
                ms.append((cb * decay).astype(BF16))
            xp = xdt[:, pcols]
            rhs = jnp.concatenate([jnp.where(lo1, xp, 0.0), jnp.where(lo1, 0.0, xp)], axis=0)
            y_diag = _dot(jnp.concatenate(ms, axis=1), rhs.astype(BF16))
            pairs.append(y_diag + y_off[:, pcols] + dskip_e[:, pcols] * xg[:, pcols])
        y = jnp.concatenate(pairs, axis=1)
        y = y * _silu(z_ref[rows, gcols].astype(F32))
        o_ref[rows, gcols] = (_rms_normalize(y) * snorm_ref[:, gcols]).astype(BF16)


def _mixmerge_kernel(sink_ref, q_ref, kc_ref, vc_ref, z_ref, xs_ref, b_ref, c_ref, dt_ref, cw_ref, cb_ref,
                     dtb_ref, alog_ref, dskip_ref, snorm_ref, e_ref, shift_ref,
                     x_ref, mod_ref, npost_ref, ga_ref, gs_ref, wa_ref, ws_ref, wo_ref, o_ref,
                     ao_ref, ys_ref, ao_prev_ref, ys_prev_ref, kk_ref, vlo_ref, vhi_ref, bias_ref,
                     raw_ref, state_ref, xc_ref, *, chunks_per_seq):
    s = pl.program_id(0)
    last = pl.num_programs(0) - 1

    def mixers():
        for sub in range(TM_MIX // CHUNK):
            rows = slice(sub * CHUNK, (sub + 1) * CHUNK)
            chunk = s * (TM_MIX // CHUNK) + sub
            first = lax.rem(chunk, chunks_per_seq) == 0
            next_first = lax.rem(chunk + 1, chunks_per_seq) == 0
            _attn_chunk(first, rows, sink_ref, q_ref, kc_ref, vc_ref, ao_ref, kk_ref, vlo_ref, vhi_ref,
                        bias_ref)
            _ssd_chunk(next_first, rows, z_ref, xs_ref, b_ref, c_ref, dt_ref, cw_ref, cb_ref, dtb_ref,
                       alog_ref, dskip_ref, snorm_ref, e_ref, shift_ref, ys_ref, raw_ref, state_ref, xc_ref)

    def merge():
        y_a = _dot(ao_prev_ref[...], wa_ref[...])
        y_s = _dot(ys_prev_ref[...], ws_ref[...])
        merged = _sigmoid(ga_ref[...].astype(F32)) * y_a + _sigmoid(gs_ref[...].astype(F32)) * y_s
        t = _dot(merged.astype(BF16), wo_ref[...])
        o_ref[...] = x_ref[...] + _rms_normalize(t) * (npost_ref[...] * mod_ref[0, 5:6, :])

    def hand_over():
        ao_prev_ref[...] = ao_ref[...]
        ys_prev_ref[...] = ys_ref[...]

    @pl.when(s == 0)
    def _():
        _attn_constants(vlo_ref, vhi_ref, bias_ref)
        kk_ref[0:WINDOW, :] = jnp.zeros((WINDOW, 2 * KV_WIDTH), BF16)
        for g in range(ATTN_KV_HEADS):
            vlo_ref[0:WINDOW, 2 * g * LANES:(2 * g + 1) * LANES] = jnp.zeros((WINDOW, LANES), BF16)
            vhi_ref[0:WINDOW, 2 * g * LANES:(2 * g + 1) * LANES] = jnp.zeros((WINDOW, LANES), BF16)
        raw_ref[0:CHUNK, :] = jnp.zeros((CHUNK, CONV_CH), BF16)
        state_ref[...] = jnp.zeros_like(state_ref)
        mixers()

    @pl.when((s > 0) & (s < last))
    def _():
        hand_over()
        merge()
        mixers()

    @pl.when(s == last)
    def _():
        hand_over()
        merge()


def _mixmerge(x, mod_l, npost, proj, dt_raw, sinks, conv_w, conv_b, dt_bias, a_log, d_skip, ssm_norm,
              expand, shift, wa, ws, wo, layer, seq):
    t = x.shape[0]
    tiles = t // TM_MIX

    def side(col):
        return lambda s: (jnp.minimum(s, tiles - 1), col)

    def lag(col):
        return lambda s: (jnp.maximum(s - 1, 0), col)

    def const(s):
        return (0, 0)

    resident = pl.Buffered(1)
    kern = functools.partial(_mixmerge_kernel, chunks_per_seq=seq // CHUNK)
    return pl.pallas_call(
        kern,
        grid=(tiles + 1,),
        in_specs=[
            pl.BlockSpec(memory_space=pltpu.SMEM),
            pl.BlockSpec((TM_MIX, ATTN_WIDTH), side(COL_Q // ATTN_WIDTH)),
            pl.BlockSpec((TM_MIX, 2 * KV_WIDTH), side(COL_KD // (2 * KV_WIDTH))),
            pl.BlockSpec((TM_MIX, 2 * KV_WIDTH), side(COL_VD // (2 * KV_WIDTH))),
            pl.BlockSpec((TM_MIX, D_INNER), side(COL_Z // D_INNER)),
            pl.BlockSpec((TM_MIX, D_INNER), side(COL_XS // D_INNER)),
            pl.BlockSpec((TM_MIX, BC_WIDTH), side(COL_B // BC_WIDTH)),
            pl.BlockSpec((TM_MIX, BC_WIDTH), side(COL_C // BC_WIDTH)),
            pl.BlockSpec((TM_MIX, LANES), side(0)),
            pl.BlockSpec((CONV_WIDTH, CONV_CH), const),
            pl.BlockSpec((1, CONV_CH), const),
            pl.BlockSpec((1, LANES), const),
            pl.BlockSpec((1, LANES), const),
            pl.BlockSpec((1, LANES), const),
            pl.BlockSpec((1, D_INNER), const),
            pl.BlockSpec((LANES, D_INNER), const),
            pl.BlockSpec(((CONV_WIDTH - 1) * CHUNK, 2 * CHUNK), const),
            pl.BlockSpec((TM_MIX, D_MODEL), lag(0)),
            pl.BlockSpec((1, N_MOD, D_MODEL), lambda s: (jnp.maximum(s - 1, 0) * TM_MIX // seq, 0, 0)),
            pl.BlockSpec((1, D_MODEL), const),
            pl.BlockSpec((TM_MIX, D_MODEL), lag(COL_GA // D_MODEL)),
            pl.BlockSpec((TM_MIX, D_MODEL), lag(COL_GS // D_MODEL)),
            pl.BlockSpec((None, ATTN_WIDTH, D_MODEL), lambda s: (layer, 0, 0), pipeline_mode=resident),
            pl.BlockSpec((None, D_INNER, D_MODEL), lambda s: (layer, 0, 0), pipeline_mode=resident),
            pl.BlockSpec((None, D_MODEL, D_MODEL), lambda s: (layer, 0, 0), pipeline_mode=resident),
        ],
        out_specs=pl.BlockSpec((TM_MIX, D_MODEL), lag(0)),
        out_shape=jax.ShapeDtypeStruct((t, D_MODEL), F32),
        scratch_shapes=[
            pltpu.VMEM((TM_MIX, ATTN_WIDTH), BF16),
            pltpu.VMEM((TM_MIX, D_INNER), BF16),
            pltpu.VMEM((TM_MIX, ATTN_WIDTH), BF16),
            pltpu.VMEM((TM_MIX, D_INNER), BF16),
            pltpu.VMEM((2 * WINDOW, 2 * KV_WIDTH), BF16),
            pltpu.VMEM((2 * WINDOW, 4 * KV_WIDTH), BF16),
            pltpu.VMEM((2 * WINDOW, 4 * KV_WIDTH), BF16),
            pltpu.VMEM((2, ATTN_HEADS, WINDOW, 2 * WINDOW), F32),
            pltpu.VMEM((2 * CHUNK, CONV_CH), BF16),
            pltpu.VMEM((D_STATE, D_INNER), F32),
            pltpu.VMEM((CHUNK, CONV_CH), F32),
        ],
        compiler_params=pltpu.CompilerParams(
            dimension_semantics=("arbitrary",), vmem_limit_bytes=VMEM_LIMIT_MIX),
        name="mixers_merge",
    )(sinks, proj, proj, proj, proj, proj, proj, proj, dt_raw, conv_w, conv_b, dt_bias, a_log, d_skip,
      ssm_norm, expand, shift, x, mod_l, npost, proj, proj, wa, ws, wo)


def _reorder_w_in(w_in):
    o = 0
    q = w_in[..., o:o + ATTN_WIDTH]; o += ATTN_WIDTH
    k = w_in[..., o:o + KV_WIDTH]; o += KV_WIDTH
    v = w_in[..., o:o + KV_WIDTH]; o += KV_WIDTH
    z = w_in[..., o:o + D_INNER]; o += D_INNER
    xs = w_in[..., o:o + D_INNER]; o += D_INNER
    bm = w_in[..., o:o + BC_WIDTH]; o += BC_WIDTH
    cm = w_in[..., o:o + BC_WIDTH]; o += BC_WIDTH
    dt = w_in[..., o:o + SSM_HEADS]; o += SSM_HEADS
    ga = w_in[..., o:o + D_MODEL]; o += D_MODEL
    gs = w_in[..., o:o + D_MODEL]

    def dup(w):
        w4 = w.reshape(DEPTH, D_MODEL, ATTN_KV_HEADS, HEAD_DIM)
        return jnp.concatenate([w4, w4], axis=-1).reshape(DEPTH, D_MODEL, 2 * KV_WIDTH)

    main = jnp.concatenate([z, ga, gs, xs, q, bm, cm, dup(k), dup(v)], axis=-1).astype(BF16)
    dt_w = jnp.pad(dt, ((0, 0), (0, 0), (0, LANES - SSM_HEADS))).astype(BF16)
    return main, dt_w


def _pad_heads(p):
    return jnp.pad(p, ((0, 0), (0, LANES - SSM_HEADS))).reshape(DEPTH, 1, LANES)


def kernel(x, c, w_mod, b_mod, norm_pre, norm_post, w_ffn_gate, w_ffn_up, w_ffn_down, w_in,
           attn_sinks, conv_w, conv_b, dt_bias, a_log, d_skip, ssm_norm, w_attn_out, w_ssm_out, w_out):
    batch, seq, _ = x.shape
    assert seq % max(TM_FFN, TM_PROJ, TM_MIX, CHUNK) == 0 and WINDOW == CHUNK
    assert batch <= SUBLANES
    t = batch * seq

    c_pad = jnp.pad(c, ((0, SUBLANES - batch), (0, 0)))
    mod = _mod_all(c_pad, w_mod, b_mod).reshape(DEPTH, SUBLANES, N_MOD, D_MODEL)[:, :batch]

    wg = _cast_bf16(w_ffn_gate)
    wu = _cast_bf16(w_ffn_up)
    wd = _cast_bf16(w_ffn_down)
    w_main, w_dt = _reorder_w_in(w_in)
    wa = w_attn_out.astype(BF16)
    ws = w_ssm_out.astype(BF16)
    wo = w_out.astype(BF16)
    dtb = _pad_heads(dt_bias)
    alog = _pad_heads(a_log)
    dsk = _pad_heads(d_skip)
    lane_head = lax.broadcasted_iota(jnp.int32, (LANES, D_INNER), 1) // SSM_HEAD_DIM
    expand = (lane_head == lax.broadcasted_iota(jnp.int32, (LANES, D_INNER), 0)).astype(BF16)
    out_row = lax.broadcasted_iota(jnp.int32, (CONV_WIDTH - 1, CHUNK, 2 * CHUNK), 1)
    delay = lax.broadcasted_iota(jnp.int32, (CONV_WIDTH - 1, CHUNK, 2 * CHUNK), 0) + 1
    src_row = lax.broadcasted_iota(jnp.int32, (CONV_WIDTH - 1, CHUNK, 2 * CHUNK), 2)
    shift = (src_row == CHUNK + out_row - delay).astype(BF16).reshape((CONV_WIDTH - 1) * CHUNK, 2 * CHUNK)

    xf = x.reshape(t, D_MODEL)
    for l in range(DEPTH):
        npre = norm_pre[l].reshape(3, 1, D_MODEL)
        npost = norm_post[l].reshape(3, 1, D_MODEL)
        xf = _ffn(xf, mod[l], npre[0], npost[0], wg, wu, wd, l, 0, seq)
        proj, dt_raw = _inproj(xf, mod[l], npre[1], w_main, w_dt, l, seq)
        xf = _mixmerge(xf, mod[l], npost[1], proj, dt_raw, attn_sinks[l], conv_w[l],
                       conv_b[l].reshape(1, CONV_CH), dtb[l], alog[l], dsk[l],
                       ssm_norm[l].reshape(1, D_INNER), expand, shift, wa, ws, wo, l, seq)
        xf = _ffn(xf, mod[l], npre[2], npost[2], wg, wu, wd, l, 1, seq)
    return xf.reshape(batch, seq, D_MODEL)
```
